```python
import jax, jax.numpy as jnp
from jax import lax
import numpy as np

D_MODEL = 1024
BATCH = 16
SEQ = 4096
DEPTH = 2

HEAD_DIM = 64
ATTN_WIDTH = D_MODEL // 2
N_Q_HEADS = ATTN_WIDTH // HEAD_DIM
N_KV_HEADS = max(N_Q_HEADS // 4, 1)
Q_PER_KV = N_Q_HEADS // N_KV_HEADS
KV_WIDTH = N_KV_HEADS * HEAD_DIM
WINDOW = 128
BLOCK = 128

CHUNK = 128
SGU_GROUP_DIM = 128
SGU_WIDTH = D_MODEL // 2
SGU_GROUPS = SGU_WIDTH // SGU_GROUP_DIM

ALPHA = (2.0 * DEPTH) ** 0.25
BETA = (8.0 * DEPTH) ** -0.25
LN_EPS = 1e-5

SPLITS = (ATTN_WIDTH, KV_WIDTH, KV_WIDTH, ATTN_WIDTH,
          SGU_WIDTH, SGU_WIDTH, SGU_WIDTH, D_MODEL, D_MODEL)
IN_COLS = sum(SPLITS)

kernel_name = "hybrid_swa_sink_sgu_gated_deepnorm"


def _layernorm(x, g, b):
    xf = x.astype(jnp.float32)
    mu = jnp.mean(xf, axis=-1, keepdims=True)
    var = jnp.mean(jnp.square(xf - mu), axis=-1, keepdims=True)
    y = (xf - mu) * lax.rsqrt(var + LN_EPS) * g.astype(jnp.float32) + b.astype(jnp.float32)
    return y.astype(x.dtype)


def _split_cols(h):
    parts, start = [], 0
    for w in SPLITS:
        parts.append(h[..., start:start + w])
        start += w
    return parts


def _swa_sinks(q, k, v, sinks):
    B, S = q.shape[0], q.shape[1]
    nb = S // BLOCK
    qb = q.reshape(B, nb, BLOCK, N_KV_HEADS, Q_PER_KV, HEAD_DIM)
    kb = k.reshape(B, nb, BLOCK, N_KV_HEADS, HEAD_DIM)
    vb = v.reshape(B, nb, BLOCK, N_KV_HEADS, HEAD_DIM)
    zpad = jnp.zeros_like(kb[:, :1])
    kk = jnp.concatenate([jnp.concatenate([zpad, kb[:, :-1]], axis=1), kb], axis=2)
    vv = jnp.concatenate([jnp.concatenate([zpad, vb[:, :-1]], axis=1), vb], axis=2)
    scale = HEAD_DIM ** -0.5
    scores = jnp.einsum('bnqhgd,bnkhd->bnhgqk', qb, kk).astype(jnp.float32) * scale
    qpos = jnp.arange(BLOCK)[:, None] + BLOCK
    kpos = jnp.arange(2 * BLOCK)[None, :]
    band = (kpos <= qpos) & (kpos > qpos - WINDOW)
    blk = jnp.arange(nb)[:, None, None]
    valid = band[None] & ((blk > 0) | (kpos[None] >= BLOCK))
    neg = jnp.finfo(jnp.float32).min
    scores = jnp.where(valid[None, :, None, None], scores, neg)
    sink = sinks.astype(jnp.float32).reshape(N_KV_HEADS, Q_PER_KV)[None, None, :, :, None, None]
    m = jnp.maximum(jnp.max(scores, axis=-1, keepdims=True), sink)
    p = jnp.exp(scores - m)
    denom = jnp.sum(p, axis=-1, keepdims=True) + jnp.exp(sink - m)
    probs = (p / denom).astype(vv.dtype)
    out = jnp.einsum('bnhgqk,bnkhd->bnqhgd', probs, vv)
    return out.reshape(B, S, ATTN_WIDTH)


def _chunked_sgu(u, v, vn_g, vn_b, w_s, b_s):
    B, S = v.shape[0], v.shape[1]
    nc = S // CHUNK
    v = _layernorm(v, vn_g, vn_b)
    vc = v.reshape(B, nc, CHUNK, SGU_GROUPS, SGU_GROUP_DIM)
    tril = jnp.tril(jnp.ones((CHUNK, CHUNK), dtype=w_s.dtype))
    w = w_s * tril[None]
    mixed = jnp.einsum('gts,bcsgd->bctgd', w, vc) + b_s.T[None, None, :, :, None]
    return u * mixed.reshape(B, S, SGU_WIDTH)


def setup_inputs(seed: int = 0) -> dict:
    key = jax.random.key(seed)
    ks = jax.random.split(key, 16)
    L, D = DEPTH, D_MODEL
    x = jax.random.normal(ks[0], (BATCH, SEQ, D), jnp.float32)
    ln_in_g = 1.0 + 0.05 * jax.random.normal(ks[1], (D,), jnp.float32)
    ln_in_b = 0.02 * jax.random.normal(ks[2], (D,), jnp.float32)
    col_scale = jnp.concatenate([
        jnp.ones((ATTN_WIDTH + KV_WIDTH,), jnp.float32),
        jnp.full((KV_WIDTH,), BETA, jnp.float32),
        jnp.ones((ATTN_WIDTH,), jnp.float32),
        jnp.full((SGU_WIDTH,), BETA, jnp.float32),
        jnp.ones((2 * SGU_WIDTH + 2 * D,), jnp.float32)])
    w_in = jax.random.normal(ks[3], (L, D, IN_COLS), jnp.float32) * (D ** -0.5) * col_scale
    b_in = 0.02 * jax.random.normal(ks[4], (L, IN_COLS), jnp.float32)
    sinks = 0.5 * jax.random.normal(ks[5], (L, N_Q_HEADS), jnp.float32)
    vn_g = 1.0 + 0.05 * jax.random.normal(ks[6], (L, SGU_WIDTH), jnp.float32)
    vn_b = 0.02 * jax.random.normal(ks[7], (L, SGU_WIDTH), jnp.float32)
    w_s = jax.random.normal(ks[8], (L, SGU_GROUPS, CHUNK, CHUNK), jnp.float32) * (CHUNK ** -0.5)
    b_s = 1.0 + 0.1 * jax.random.normal(ks[9], (L, SGU_GROUPS, CHUNK), jnp.float32)
    p_a = jax.random.normal(ks[10], (L, ATTN_WIDTH, D), jnp.float32) * (ATTN_WIDTH ** -0.5) * BETA
    p_b = jax.random.normal(ks[11], (L, SGU_WIDTH, D), jnp.float32) * (SGU_WIDTH ** -0.5) * BETA
    w_out = jax.random.normal(ks[12], (L, D, D), jnp.float32) * (D ** -0.5) * BETA
    b_out = 0.02 * jax.random.normal(ks[13], (L, D), jnp.float32)
    ln_g = 1.0 + 0.05 * jax.random.normal(ks[14], (L, D), jnp.float32)
    ln_b = 0.02 * jax.random.normal(ks[15], (L, D), jnp.float32)
    return {"x": x, "ln_in_g": ln_in_g, "ln_in_b": ln_in_b, "w_in": w_in, "b_in": b_in,
            "sinks": sinks, "vn_g": vn_g, "vn_b": vn_b, "w_s": w_s, "b_s": b_s,
            "p_a": p_a, "p_b": p_b, "w_out": w_out, "b_out": b_out,
            "ln_g": ln_g, "ln_b": ln_b}


def reference(x, ln_in_g, ln_in_b, w_in, b_in, sinks, vn_g, vn_b, w_s, b_s,
              p_a, p_b, w_out, b_out, ln_g, ln_b):
    x = _layernorm(x, ln_in_g, ln_in_b)
    for l in range(DEPTH):
        h = jnp.einsum('bsd,dc->bsc', x, w_in[l]) + b_in[l]
        q, k, v, g_a, u_b, v_b, g_b, r_a, r_b = _split_cols(h)
        y_a = _swa_sinks(q, k, v, sinks[l]) * jax.nn.silu(g_a)
        y_b = _chunked_sgu(jax.nn.gelu(u_b), jax.nn.gelu(v_b), vn_g[l], vn_b[l],
                           w_s[l], b_s[l]) * jax.nn.silu(g_b)
        merged = (jax.nn.sigmoid(r_a) * jnp.einsum('bsc,cd->bsd', y_a, p_a[l])
                  + jax.nn.sigmoid(r_b) * jnp.einsum('bsc,cd->bsd', y_b, p_b[l]))
        out = jnp.einsum('bsd,de->bse', merged, w_out[l]) + b_out[l]
        x = _layernorm(ALPHA * x + out, ln_g[l], ln_b[l])
    return x
```

```python
import functools

import jax
import jax.numpy as jnp
from jax import lax
from jax.experimental import pallas as pl
from jax.experimental.pallas import tpu as pltpu

D_MODEL = 1024
DEPTH = 2
HEAD_DIM = 64
ATTN_WIDTH = 512
KV_WIDTH = 128
BLOCK = 128
SGU_WIDTH = 512
SGU_GROUPS = 4
ALPHA = (2.0 * DEPTH) ** 0.25
LN_EPS = 1e-5
LANES = 128
PAIRS = ATTN_WIDTH // LANES

_SPLITS = (ATTN_WIDTH, KV_WIDTH, KV_WIDTH, ATTN_WIDTH, SGU_WIDTH, SGU_WIDTH, SGU_WIDTH,
           D_MODEL, D_MODEL)
_OFFS = tuple(sum(_SPLITS[:i]) for i in range(len(_SPLITS) + 1))
IN_COLS = _OFFS[-1]
(_Q, _K, _V, _GA, _UB, _VB, _GB, _RA, _RB) = tuple(
    (_OFFS[i], _OFFS[i + 1]) for i in range(len(_SPLITS)))

SEQ_TILE = 512
VMEM_LIMIT_BYTES = 56 * 1024 * 1024

_F32 = jnp.float32
_BF16 = jnp.bfloat16
_NEG = float(jnp.finfo(jnp.float32).min)


def _layernorm(x, g, b):
    mu = jnp.mean(x, axis=-1, keepdims=True)
    xc = x - mu
    var = jnp.mean(xc * xc, axis=-1, keepdims=True)
    return xc * lax.rsqrt(var + LN_EPS) * g + b


def _sigmoid(x):
    return 0.5 * jnp.tanh(0.5 * x) + 0.5


def _silu(x):
    return x * _sigmoid(x)


def _gelu_tanh(x):
    c = 0.7978845608028654
    return 0.5 * x * (1.0 + jnp.tanh(c * (x + 0.044715 * (x * x * x))))


def _layer_kernel(sinks_ref, x_ref, lng_ref, lnb_ref, win_ref, bin_ref, vng_ref, vnb_ref,
                  ws_ref, bs_ref, pa_ref, pb_ref, wout_ref, bout_ref, g_ref, b_ref,
                  o_ref,
                  xn_s, xb_s, q_s, k_s, v_s, att_s, u_s, vn_s,
                  *, apply_in_ln, ts):
    nblk = ts // BLOCK
    first_tile = pl.program_id(1) == 0

    x = x_ref[0]
    if apply_in_ln:
        x = _layernorm(x, lng_ref[...], lnb_ref[...])
    xn_s[...] = x
    xb_s[...] = x.astype(_BF16)

    def proj(rng):
        lo, hi = rng
        return (jnp.dot(xb_s[...], win_ref[:, lo:hi], preferred_element_type=_F32)
                + bin_ref[:, lo:hi])

    lane = lax.broadcasted_iota(jnp.int32, (ts, LANES), 1)
    left = lane < HEAD_DIM

    q_s[...] = (proj(_Q) * (HEAD_DIM ** -0.5)).astype(_BF16)

    @pl.when(first_tile)
    def _():
        k_s[:, 0:BLOCK, :] = jnp.zeros((4, BLOCK, LANES), _BF16)
        v_s[:, 0:BLOCK, :] = jnp.zeros((4, BLOCK, LANES), _BF16)

    @pl.when(jnp.logical_not(first_tile))
    def _():
        k_s[:, 0:BLOCK, :] = k_s[:, ts:ts + BLOCK, :]
        v_s[:, 0:BLOCK, :] = v_s[:, ts:ts + BLOCK, :]

    def spread(t, dst):
        rot = pltpu.roll(t, HEAD_DIM, axis=1)
        zero = jnp.zeros_like(t)
        dst[0, BLOCK:BLOCK + ts, :] = jnp.where(left, t, zero).astype(_BF16)
        dst[1, BLOCK:BLOCK + ts, :] = jnp.where(left, zero, rot).astype(_BF16)
        dst[2, BLOCK:BLOCK + ts, :] = jnp.where(left, rot, zero).astype(_BF16)
        dst[3, BLOCK:BLOCK + ts, :] = jnp.where(left, zero, t).astype(_BF16)

    spread(proj(_K), k_s)
    spread(proj(_V), v_s)

    row = lax.broadcasted_iota(jnp.int32, (BLOCK, BLOCK), 0)
    col = lax.broadcasted_iota(jnp.int32, (BLOCK, BLOCK), 1)
    upper = col > row
    left_b = col < HEAD_DIM

    for j in range(nblk):
        r0 = j * BLOCK
        for p in range(PAIRS):
            h = p // 2
            q2 = q_s[r0:r0 + BLOCK, p * LANES:(p + 1) * LANES]
            acc = None
            inv = []
            for par in range(2):
                kk = k_s[2 * h + par, r0:r0 + 2 * BLOCK, :]
                vv = v_s[2 * h + par, r0:r0 + 2 * BLOCK, :]
                s = lax.dot_general(q2, kk, (((1,), (1,)), ((), ())),
                                    preferred_element_type=_F32)
                t = jnp.where(upper, s[:, 0:BLOCK], s[:, BLOCK:2 * BLOCK])
                if j == 0:
                    t = jnp.where(jnp.logical_and(upper, first_tile), _NEG, t)
                sink = sinks_ref[2 * p + par]
                m = jnp.maximum(jnp.max(t, axis=-1, keepdims=True), sink)
                e = jnp.exp(t - m)
                den = jnp.sum(e, axis=-1, keepdims=True) + jnp.exp(sink - m)
                inv.append(1.0 / den)
                zero = jnp.zeros_like(e)
                pcat = jnp.concatenate([jnp.where(upper, e, zero), jnp.where(upper, zero, e)],
                                       axis=1).astype(_BF16)
                o = jnp.dot(pcat, vv, preferred_element_type=_F32)
                acc = o if acc is None else acc + o
            att_s[r0:r0 + BLOCK, p * LANES:(p + 1) * LANES] = acc * jnp.where(
                left_b, inv[0], inv[1])

    ya = (att_s[...] * _silu(proj(_GA))).astype(_BF16)
    pa = jnp.dot(ya, pa_ref[...], preferred_element_type=_F32)
    merged = _sigmoid(proj(_RA)) * pa

    u_s[...] = _gelu_tanh(proj(_UB))
    vn_s[...] = _layernorm(_gelu_tanh(proj(_VB)), vng_ref[...], vnb_ref[...]).astype(_BF16)
    tril = col <= row
    for g in range(SGU_GROUPS):
        w = jnp.where(tril, ws_ref[g], 0.0).astype(_BF16)
        bias = bs_ref[g]
        c0 = g * LANES
        for c in range(nblk):
            r0 = c * BLOCK
            mixed = jnp.dot(w, vn_s[r0:r0 + BLOCK, c0:c0 + LANES],
                            preferred_element_type=_F32) + bias
            u_s[r0:r0 + BLOCK, c0:c0 + LANES] = u_s[r0:r0 + BLOCK, c0:c0 + LANES] * mixed
    yb = (u_s[...] * _silu(proj(_GB))).astype(_BF16)
    pb = jnp.dot(yb, pb_ref[...], preferred_element_type=_F32)
    merged = merged + _sigmoid(proj(_RB)) * pb

    out = jnp.dot(merged.astype(_BF16), wout_ref[...], preferred_element_type=_F32)
    z = ALPHA * xn_s[...] + out + bout_ref[...]
    o_ref[0] = _layernorm(z, g_ref[...], b_ref[...])


def _resident(shape):
    nd = len(shape)
    return pl.BlockSpec(shape, lambda b, s: (0,) * nd, pipeline_mode=pl.Buffered(1))


def _layer(x, sinks, ln_in_g, ln_in_b, w_in, b_in, vn_g, vn_b, w_s, b_s, p_a, p_b, w_out,
           b_out, ln_g, ln_b, *, apply_in_ln):
    batch, seq, d = x.shape
    ts = SEQ_TILE
    assert d == D_MODEL and seq % ts == 0 and ts % BLOCK == 0
    kernel = functools.partial(_layer_kernel, apply_in_ln=apply_in_ln, ts=ts)
    tile = pl.BlockSpec((1, ts, d), lambda b, s: (b, s, 0))
    in_specs = [
        pl.BlockSpec(memory_space=pltpu.SMEM),
        tile,
        _resident((1, d)), _resident((1, d)),
        _resident((d, IN_COLS)), _resident((1, IN_COLS)),
        _resident((1, SGU_WIDTH)), _resident((1, SGU_WIDTH)),
        _resident((SGU_GROUPS, BLOCK, BLOCK)), _resident((SGU_GROUPS, BLOCK, LANES)),
        _resident((ATTN_WIDTH, d)), _resident((SGU_WIDTH, d)),
        _resident((d, d)), _resident((1, d)),
        _resident((1, d)), _resident((1, d)),
    ]
    scratch = [
        pltpu.VMEM((ts, d), _F32),
        pltpu.VMEM((ts, d), _BF16),
        pltpu.VMEM((ts, ATTN_WIDTH), _BF16),
        pltpu.VMEM((4, BLOCK + ts, LANES), _BF16),
        pltpu.VMEM((4, BLOCK + ts, LANES), _BF16),
        pltpu.VMEM((ts, ATTN_WIDTH), _F32),
        pltpu.VMEM((ts, SGU_WIDTH), _F32),
        pltpu.VMEM((ts, SGU_WIDTH), _BF16),
    ]
    return pl.pallas_call(
        kernel,
        grid=(batch, seq // ts),
        in_specs=in_specs,
        out_specs=tile,
        out_shape=jax.ShapeDtypeStruct(x.shape, x.dtype),
        scratch_shapes=scratch,
        compiler_params=pltpu.CompilerParams(
            dimension_semantics=("arbitrary", "arbitrary"),
            vmem_limit_bytes=VMEM_LIMIT_BYTES),
        name="hybrid_layer_in_ln" if apply_in_ln else "hybrid_layer",
    )(sinks, x, ln_in_g, ln_in_b, w_in, b_in, vn_g, vn_b, w_s, b_s, p_a, p_b, w_out, b_out,
      ln_g, ln_b)


def kernel(x, ln_in_g, ln_in_b, w_in, b_in, sinks, vn_g, vn_b, w_s, b_s, p_a, p_b, w_out,
           b_out, ln_g, ln_b):
    row = lambda a: a.reshape(1, -1)
    for l in range(DEPTH):
        bias = jnp.broadcast_to(b_s[l][:, :, None], (SGU_GROUPS, BLOCK, LANES))
        x = _layer(x, sinks[l], row(ln_in_g), row(ln_in_b),
                   w_in[l].astype(_BF16), row(b_in[l]), row(vn_g[l]), row(vn_b[l]),
                   w_s[l], bias, p_a[l].astype(_BF16), p_b[l].astype(_BF16),
                   w_out[l].astype(_BF16), row(b_out[l]), row(ln_g[l]), row(ln_b[l]),
                   apply_in_ln=(l == 0))
    return x
```

```python
import functools

import jax
import jax.numpy as jnp
from jax import lax
from jax.experimental import pallas as pl
from jax.experimental.pallas import tpu as pltpu

D_MODEL = 1024
DEPTH = 2
HEAD_DIM = 64
ATTN_WIDTH = 512
KV_WIDTH = 128
BLOCK = 128
SGU_WIDTH = 512
SGU_GROUPS = 4
ALPHA = (2.0 * DEPTH) ** 0.25
LN_EPS = 1e-5
LANES = 128
PAIRS = ATTN_WIDTH // LANES

_SPLITS = (ATTN_WIDTH, KV_WIDTH, KV_WIDTH, ATTN_WIDTH, SGU_WIDTH, SGU_WIDTH, SGU_WIDTH,
           D_MODEL, D_MODEL)
_OFFS = tuple(sum(_SPLITS[:i]) for i in range(len(_SPLITS) + 1))
IN_COLS = _OFFS[-1]
(_Q, _K, _V, _GA, _UB, _VB, _GB, _RA, _RB) = tuple(
    (_OFFS[i], _OFFS[i + 1]) for i in range(len(_SPLITS)))

SEQ_TILE = 512
VMEM_LIMIT_BYTES = 56 * 1024 * 1024

_F32 = jnp.float32
_BF16 = jnp.bfloat16
_NEG = float(jnp.finfo(jnp.float32).min)


def _layernorm(x, g, b):
    mu = jnp.mean(x, axis=-1, keepdims=True)
    xc = x - mu
    var = jnp.mean(xc * xc, axis=-1, keepdims=True)
    return xc * lax.rsqrt(var + LN_EPS) * g + b


def _sigmoid(x):
    return 0.5 * jnp.tanh(0.5 * x) + 0.5


def _silu(x):
    return x * _sigmoid(x)


def _gelu_tanh(x):
    c = 0.7978845608028654
    return 0.5 * x * (1.0 + jnp.tanh(c * (x + 0.044715 * (x * x * x))))


def _layer_kernel(sinks_ref, x_ref, lng_ref, lnb_ref, win_ref, bin_ref, vng_ref, vnb_ref,
                  ws_ref, bs_ref, pa_ref, pb_ref, wout_ref, bout_ref, g_ref, b_ref,
                  o_ref,
                  xn_s, xb_s, q_s, k_s, v_s, t_s, p_s, att_s, u_s, vb_s, vn_s, m_s, mb_s,
                  *, apply_in_ln, ts):
    nblk = ts // BLOCK
    first_tile = pl.program_id(1) == 0

    x = x_ref[0]
    if apply_in_ln:
        x = _layernorm(x, lng_ref[...], lnb_ref[...])
    xn_s[...] = x
    xb_s[...] = x.astype(_BF16)

    def proj(rng):
        lo, hi = rng
        return (jnp.dot(xb_s[...], win_ref[:, lo:hi], preferred_element_type=_F32)
                + bin_ref[:, lo:hi])

    lane = lax.broadcasted_iota(jnp.int32, (ts, LANES), 1)
    left = lane < HEAD_DIM

    q_s[...] = (proj(_Q) * (HEAD_DIM ** -0.5)).astype(_BF16)

    @pl.when(first_tile)
    def _():
        k_s[:, 0:BLOCK, :] = jnp.zeros((4, BLOCK, LANES), _BF16)
        v_s[:, 0:BLOCK, :] = jnp.zeros((4, BLOCK, LANES), _BF16)

    @pl.when(jnp.logical_not(first_tile))
    def _():
        k_s[:, 0:BLOCK, :] = k_s[:, ts:ts + BLOCK, :]
        v_s[:, 0:BLOCK, :] = v_s[:, ts:ts + BLOCK, :]

    def spread(t, dst):
        rot = pltpu.roll(t, HEAD_DIM, axis=1)
        zero = jnp.zeros_like(t)
        dst[0, BLOCK:BLOCK + ts, :] = jnp.where(left, t, zero).astype(_BF16)
        dst[1, BLOCK:BLOCK + ts, :] = jnp.where(left, zero, rot).astype(_BF16)
        dst[2, BLOCK:BLOCK + ts, :] = jnp.where(left, rot, zero).astype(_BF16)
        dst[3, BLOCK:BLOCK + ts, :] = jnp.where(left, zero, t).astype(_BF16)

    spread(proj(_K), k_s)
    spread(proj(_V), v_s)

    row = lax.broadcasted_iota(jnp.int32, (BLOCK, BLOCK), 0)
    col = lax.broadcasted_iota(jnp.int32, (BLOCK, BLOCK), 1)
    upper = col > row

    def scores():
        for j in range(nblk):
            r0 = j * BLOCK
            for p in range(PAIRS):
                q2 = q_s[r0:r0 + BLOCK, p * LANES:(p + 1) * LANES]
                for par in range(2):
                    kk = k_s[2 * (p // 2) + par, r0:r0 + 2 * BLOCK, :]
                    s = lax.dot_general(q2, kk, (((1,), (1,)), ((), ())),
                                        preferred_element_type=_F32)
                    t = jnp.where(upper, s[:, 0:BLOCK], s[:, BLOCK:2 * BLOCK])
                    if j == 0:
                        t = jnp.where(jnp.logical_and(upper, first_tile), _NEG, t)
                    t_s[2 * p + par, r0:r0 + BLOCK, :] = t

    trow = lax.broadcasted_iota(jnp.int32, (ts, BLOCK), 0) & (BLOCK - 1)
    tcol = lax.broadcasted_iota(jnp.int32, (ts, BLOCK), 1)
    upper_t = tcol > trow

    def softmax(hd):
        t = t_s[hd]
        sink = sinks_ref[hd]
        m = jnp.maximum(jnp.max(t, axis=-1, keepdims=True), sink)
        e = jnp.exp(t - m)
        den = jnp.sum(e, axis=-1, keepdims=True) + jnp.exp(sink - m)
        pn = e * (1.0 / den)
        zero = jnp.zeros_like(pn)
        p_s[hd, :, 0:BLOCK] = jnp.where(upper_t, pn, zero).astype(_BF16)
        p_s[hd, :, BLOCK:2 * BLOCK] = jnp.where(upper_t, zero, pn).astype(_BF16)

    def weighted_values():
        for j in range(nblk):
            r0 = j * BLOCK
            for p in range(PAIRS):
                acc = None
                for par in range(2):
                    vv = v_s[2 * (p // 2) + par, r0:r0 + 2 * BLOCK, :]
                    o = jnp.dot(p_s[2 * p + par, r0:r0 + BLOCK, :], vv,
                                preferred_element_type=_F32)
                    acc = o if acc is None else acc + o
                att_s[r0:r0 + BLOCK, p * LANES:(p + 1) * LANES] = acc

    def spatial_gate():
        tril = col <= row
        for g in range(SGU_GROUPS):
            w = jnp.where(tril, ws_ref[g], 0.0).astype(_BF16)
            bias = bs_ref[g]
            c0 = g * LANES
            for c in range(nblk):
                r0 = c * BLOCK
                mixed = jnp.dot(w, vn_s[r0:r0 + BLOCK, c0:c0 + LANES],
                                preferred_element_type=_F32) + bias
                u_s[r0:r0 + BLOCK, c0:c0 + LANES] = u_s[r0:r0 + BLOCK, c0:c0 + LANES] * mixed

    scores()
    vb_s[...] = _gelu_tanh(proj(_VB))
    for hd in range(PAIRS):
        softmax(hd)
    u_s[...] = _gelu_tanh(proj(_UB))
    for hd in range(PAIRS, 2 * PAIRS):
        softmax(hd)
    weighted_values()
    ga = _silu(proj(_GA))
    vn_s[...] = _layernorm(vb_s[...], vng_ref[...], vnb_ref[...]).astype(_BF16)
    ya = (att_s[...] * ga).astype(_BF16)
    pa = jnp.dot(ya, pa_ref[...], preferred_element_type=_F32)
    m_s[...] = _sigmoid(proj(_RA)) * pa
    spatial_gate()
    yb = (u_s[...] * _silu(proj(_GB))).astype(_BF16)
    pb = jnp.dot(yb, pb_ref[...], preferred_element_type=_F32)
    mb_s[...] = (m_s[...] + _sigmoid(proj(_RB)) * pb).astype(_BF16)

    half = ts // 2
    for r0 in (0, half):
        out = jnp.dot(mb_s[r0:r0 + half, :], wout_ref[...], preferred_element_type=_F32)
        z = ALPHA * xn_s[r0:r0 + half, :] + out + bout_ref[...]
        o_ref[0, r0:r0 + half, :] = _layernorm(z, g_ref[...], b_ref[...])


def _resident(shape):
    nd = len(shape)
    return pl.BlockSpec(shape, lambda b, s: (0,) * nd, pipeline_mode=pl.Buffered(1))


def _layer(x, sinks, ln_in_g, ln_in_b, w_in, b_in, vn_g, vn_b, w_s, b_s, p_a, p_b, w_out,
           b_out, ln_g, ln_b, *, apply_in_ln):
    batch, seq, d = x.shape
    ts = SEQ_TILE
    assert d == D_MODEL and seq % ts == 0 and ts % BLOCK == 0
    kernel = functools.partial(_layer_kernel, apply_in_ln=apply_in_ln, ts=ts)
    tile = pl.BlockSpec((1, ts, d), lambda b, s: (b, s, 0))
    in_specs = [
        pl.BlockSpec(memory_space=pltpu.SMEM),
        tile,
        _resident((1, d)), _resident((1, d)),
        _resident((d, IN_COLS)), _resident((1, IN_COLS)),
        _resident((1, SGU_WIDTH)), _resident((1, SGU_WIDTH)),
        _resident((SGU_GROUPS, BLOCK, BLOCK)), _resident((SGU_GROUPS, BLOCK, LANES)),
        _resident((ATTN_WIDTH, d)), _resident((SGU_WIDTH, d)),
        _resident((d, d)), _resident((1, d)),
        _resident((1, d)), _resident((1, d)),
    ]
    scratch = [
        pltpu.VMEM((ts, d), _F32),
        pltpu.VMEM((ts, d), _BF16),
        pltpu.VMEM((ts, ATTN_WIDTH), _BF16),
        pltpu.VMEM((4, BLOCK + ts, LANES), _BF16),
        pltpu.VMEM((4, BLOCK + ts, LANES), _BF16),
        pltpu.VMEM((2 * PAIRS, ts, BLOCK), _F32),
        pltpu.VMEM((2 * PAIRS, ts, 2 * BLOCK), _BF16),
        pltpu.VMEM((ts, ATTN_WIDTH), _F32),
        pltpu.VMEM((ts, SGU_WIDTH), _F32),
        pltpu.VMEM((ts, SGU_WIDTH), _F32),
        pltpu.VMEM((ts, SGU_WIDTH), _BF16),
        pltpu.VMEM((ts, d), _F32),
        pltpu.VMEM((ts, d), _BF16),
    ]
    return pl.pallas_call(
        kernel,
        grid=(batch, seq // ts),
        in_specs=in_specs,
        out_specs=tile,
        out_shape=jax.ShapeDtypeStruct(x.shape, x.dtype),
        scratch_shapes=scratch,
        compiler_params=pltpu.CompilerParams(
            dimension_semantics=("arbitrary", "arbitrary"),
            vmem_limit_bytes=VMEM_LIMIT_BYTES),
        name="hybrid_layer_in_ln" if apply_in_ln else "hybrid_layer",
    )(sinks, x, ln_in_g, ln_in_b, w_in, b_in, vn_g, vn_b, w_s, b_s, p_a, p_b, w_out, b_out,
      ln_g, ln_b)


def kernel(x, ln_in_g, ln_in_b, w_in, b_in, sinks, vn_g, vn_b, w_s, b_s, p_a, p_b, w_out,
           b_out, ln_g, ln_b):
    row = lambda a: a.reshape(1, -1)
    for l in range(DEPTH):
        bias = jnp.broadcast_to(b_s[l][:, :, None], (SGU_GROUPS, BLOCK, LANES))
        x = _layer(x, sinks[l], row(ln_in_g), row(ln_in_b),
                   w_in[l].astype(_BF16), row(b_in[l]), row(vn_g[l]), row(vn_b[l]),
                   w_s[l], bias, p_a[l].astype(_BF16), p_b[l].astype(_BF16),
                   w_out[l].astype(_BF16), row(b_out[l]), row(ln_g[l]), row(ln_b[l]),
                   apply_in_ln=(l == 0))
    return x
```

```python
import functools

import jax
import jax.numpy as jnp
from jax import lax
from jax.experimental import pallas as pl
from jax.experimental.pallas import tpu as pltpu

D_MODEL = 1024
DEPTH = 2
HEAD_DIM = 64
ATTN_WIDTH = 512
KV_WIDTH = 128
BLOCK = 128
SGU_WIDTH = 512
SGU_GROUPS = 4
ALPHA = (2.0 * DEPTH) ** 0.25
LN_EPS = 1e-5
LANES = 128
PAIRS = ATTN_WIDTH // LANES

_SPLITS = (ATTN_WIDTH, KV_WIDTH, KV_WIDTH, ATTN_WIDTH, SGU_WIDTH, SGU_WIDTH, SGU_WIDTH,
           D_MODEL, D_MODEL)
_OFFS = tuple(sum(_SPLITS[:i]) for i in range(len(_SPLITS) + 1))
IN_COLS = _OFFS[-1]
(_Q, _K, _V, _GA, _UB, _VB, _GB, _RA, _RB) = tuple(
    (_OFFS[i], _OFFS[i + 1]) for i in range(len(_SPLITS)))

SEQ_TILE = 512
VMEM_LIMIT_BYTES = 56 * 1024 * 1024

_F32 = jnp.float32
_BF16 = jnp.bfloat16
_NEG = float(jnp.finfo(jnp.float32).min)


def _layernorm(x, g, b):
    mu = jnp.mean(x, axis=-1, keepdims=True)
    xc = x - mu
    var = jnp.mean(xc * xc, axis=-1, keepdims=True)
    return xc * lax.rsqrt(var + LN_EPS) * g + b


def _sigmoid(x):
    return 0.5 * jnp.tanh(0.5 * x) + 0.5


def _silu(x):
    return x * _sigmoid(x)


def _gelu_tanh(x):
    c = 0.7978845608028654
    return 0.5 * x * (1.0 + jnp.tanh(c * (x + 0.044715 * (x * x * x))))


def _layer_kernel(sinks_ref, x_ref, lng_ref, lnb_ref, win_ref, bin_ref, vng_ref, vnb_ref,
                  ws_ref, bs_ref, pa_ref, pb_ref, wout_ref, bout_ref, g_ref, b_ref,
                  o_ref,
                  xn_s, xb_s, q_s, k_s, v_s, t_s, p_s, att_s, u_s, vb_s, vn_s, m_s, r_s, mb_s,
                  *, apply_in_ln, ts, tiles_per_seq, n_tiles):
    step = pl.program_id(0)
    first_tile = lax.rem(step, tiles_per_seq) == 0
    slot = lax.rem(step, 2)

    @pl.when(step == 0)
    def _():
        mb_s[...] = jnp.zeros(mb_s.shape, _BF16)
        xn_s[1] = jnp.zeros(xn_s.shape[1:], _F32)
        k_s[:, 0:BLOCK, :] = jnp.zeros((4, BLOCK, LANES), _BF16)
        v_s[:, 0:BLOCK, :] = jnp.zeros((4, BLOCK, LANES), _BF16)

    x = x_ref[0]
    if apply_in_ln:
        x = _layernorm(x, lng_ref[...], lnb_ref[...])
    xn_s[slot] = x
    xb_s[...] = x.astype(_BF16)

    def finish_previous(r0, rows):
        out = jnp.dot(mb_s[r0:r0 + rows, :], wout_ref[...], preferred_element_type=_F32)
        z = ALPHA * xn_s[1 - slot, r0:r0 + rows, :] + out + bout_ref[...]
        o_ref[0, r0:r0 + rows, :] = _layernorm(z, g_ref[...], b_ref[...])

    def proj(rng):
        lo, hi = rng
        return (jnp.dot(xb_s[...], win_ref[:, lo:hi], preferred_element_type=_F32)
                + bin_ref[:, lo:hi])

    lane = lax.broadcasted_iota(jnp.int32, (ts, LANES), 1)
    left = lane < HEAD_DIM

    def spread(t, dst):
        rot = pltpu.roll(t, HEAD_DIM, axis=1)
        zero = jnp.zeros_like(t)
        dst[0, BLOCK:BLOCK + ts, :] = jnp.where(left, t, zero).astype(_BF16)
        dst[1, BLOCK:BLOCK + ts, :] = jnp.where(left, zero, rot).astype(_BF16)
        dst[2, BLOCK:BLOCK + ts, :] = jnp.where(left, rot, zero).astype(_BF16)
        dst[3, BLOCK:BLOCK + ts, :] = jnp.where(left, zero, t).astype(_BF16)

    half = ts // 2
    finish_previous(0, half)
    spread(proj(_K), k_s)
    spread(proj(_V), v_s)
    finish_previous(half, half)
    q_s[...] = (proj(_Q) * (HEAD_DIM ** -0.5)).astype(_BF16)
    m_s[...] = _sigmoid(proj(_RA))

    pl.when(step < n_tiles)(functools.partial(
        _branches, sinks_ref, win_ref, bin_ref, vng_ref, vnb_ref, ws_ref, bs_ref, pa_ref,
        pb_ref, xb_s, q_s, k_s, v_s, t_s, p_s, att_s, u_s, vb_s, vn_s, m_s, r_s, mb_s,
        first_tile=first_tile, ts=ts))


def _branches(sinks_ref, win_ref, bin_ref, vng_ref, vnb_ref, ws_ref, bs_ref, pa_ref, pb_ref,
              xb_s, q_s, k_s, v_s, t_s, p_s, att_s, u_s, vb_s, vn_s, m_s, r_s, mb_s,
              *, first_tile, ts):
    nblk = ts // BLOCK

    def proj(rng):
        lo, hi = rng
        return (jnp.dot(xb_s[...], win_ref[:, lo:hi], preferred_element_type=_F32)
                + bin_ref[:, lo:hi])

    row = lax.broadcasted_iota(jnp.int32, (BLOCK, BLOCK), 0)
    col = lax.broadcasted_iota(jnp.int32, (BLOCK, BLOCK), 1)
    upper = col > row

    def scores():
        for j in range(nblk):
            r0 = j * BLOCK
            for p in range(PAIRS):
                q2 = q_s[r0:r0 + BLOCK, p * LANES:(p + 1) * LANES]
                for par in range(2):
                    kk = k_s[2 * (p // 2) + par, r0:r0 + 2 * BLOCK, :]
                    s = lax.dot_general(q2, kk, (((1,), (1,)), ((), ())),
                                        preferred_element_type=_F32)
                    t = jnp.where(upper, s[:, 0:BLOCK], s[:, BLOCK:2 * BLOCK])
                    if j == 0:
                        t = jnp.where(jnp.logical_and(upper, first_tile), _NEG, t)
                    t_s[2 * p + par, r0:r0 + BLOCK, :] = t

    trow = lax.broadcasted_iota(jnp.int32, (ts, BLOCK), 0) & (BLOCK - 1)
    tcol = lax.broadcasted_iota(jnp.int32, (ts, BLOCK), 1)
    upper_t = tcol > trow

    def softmax(hd):
        t = t_s[hd]
        sink = sinks_ref[hd]
        m = jnp.maximum(jnp.max(t, axis=-1, keepdims=True), sink)
        e = jnp.exp(t - m)
        den = jnp.sum(e, axis=-1, keepdims=True) + jnp.exp(sink - m)
        pn = e * (1.0 / den)
        zero = jnp.zeros_like(pn)
        p_s[hd, :, 0:BLOCK] = jnp.where(upper_t, pn, zero).astype(_BF16)
        p_s[hd, :, BLOCK:2 * BLOCK] = jnp.where(upper_t, zero, pn).astype(_BF16)

    def weighted_values():
        for j in range(nblk):
            r0 = j * BLOCK
            for p in range(PAIRS):
                acc = None
                for par in range(2):
                    vv = v_s[2 * (p // 2) + par, r0:r0 + 2 * BLOCK, :]
                    o = jnp.dot(p_s[2 * p + par, r0:r0 + BLOCK, :], vv,
                                preferred_element_type=_F32)
                    acc = o if acc is None else acc + o
                att_s[r0:r0 + BLOCK, p * LANES:(p + 1) * LANES] = acc

    def spatial_gate():
        tril = col <= row
        for g in range(SGU_GROUPS):
            w = jnp.where(tril, ws_ref[g], 0.0).astype(_BF16)
            bias = bs_ref[g]
            c0 = g * LANES
            for c in range(nblk):
                r0 = c * BLOCK
                mixed = jnp.dot(w, vn_s[r0:r0 + BLOCK, c0:c0 + LANES],
                                preferred_element_type=_F32) + bias
                u_s[r0:r0 + BLOCK, c0:c0 + LANES] = u_s[r0:r0 + BLOCK, c0:c0 + LANES] * mixed

    scores()
    vb_s[...] = _gelu_tanh(proj(_VB))
    for hd in range(PAIRS):
        softmax(hd)
    u_s[...] = _gelu_tanh(proj(_UB))
    for hd in range(PAIRS, 2 * PAIRS):
        softmax(hd)
    weighted_values()
    k_s[:, 0:BLOCK, :] = k_s[:, ts:ts + BLOCK, :]
    v_s[:, 0:BLOCK, :] = v_s[:, ts:ts + BLOCK, :]
    ga = _silu(proj(_GA))
    vn_s[...] = _layernorm(vb_s[...], vng_ref[...], vnb_ref[...]).astype(_BF16)
    ya = (att_s[...] * ga).astype(_BF16)
    pa = jnp.dot(ya, pa_ref[...], preferred_element_type=_F32)
    m_s[...] = m_s[...] * pa
    r_s[...] = _sigmoid(proj(_RB))
    spatial_gate()
    yb = (u_s[...] * _silu(proj(_GB))).astype(_BF16)
    pb = jnp.dot(yb, pb_ref[...], preferred_element_type=_F32)
    mb_s[...] = (m_s[...] + r_s[...] * pb).astype(_BF16)


def _resident(shape):
    nd = len(shape)
    return pl.BlockSpec(shape, lambda i: (0,) * nd, pipeline_mode=pl.Buffered(1))


def _layer(x, sinks, ln_in_g, ln_in_b, w_in, b_in, vn_g, vn_b, w_s, b_s, p_a, p_b, w_out,
           b_out, ln_g, ln_b, *, apply_in_ln):
    batch, seq, d = x.shape
    ts = SEQ_TILE
    assert d == D_MODEL and seq % ts == 0 and ts % BLOCK == 0
    tiles_per_seq = seq // ts
    n_tiles = batch * tiles_per_seq
    kernel = functools.partial(_layer_kernel, apply_in_ln=apply_in_ln, ts=ts,
                               tiles_per_seq=tiles_per_seq, n_tiles=n_tiles)

    def tile_index(t):
        return (t // tiles_per_seq, t % tiles_per_seq, 0)

    x_spec = pl.BlockSpec((1, ts, d), lambda i: tile_index(jnp.minimum(i, n_tiles - 1)))
    o_spec = pl.BlockSpec((1, ts, d), lambda i: tile_index(jnp.maximum(i - 1, 0)))
    in_specs = [
        pl.BlockSpec(memory_space=pltpu.SMEM),
        x_spec,
        _resident((1, d)), _resident((1, d)),
        _resident((d, IN_COLS)), _resident((1, IN_COLS)),
        _resident((1, SGU_WIDTH)), _resident((1, SGU_WIDTH)),
        _resident((SGU_GROUPS, BLOCK, BLOCK)), _resident((SGU_GROUPS, BLOCK, LANES)),
        _resident((ATTN_WIDTH, d)), _resident((SGU_WIDTH, d)),
        _resident((d, d)), _resident((1, d)),
        _resident((1, d)), _resident((1, d)),
    ]
    scratch = [
        pltpu.VMEM((2, ts, d), _F32),
        pltpu.VMEM((ts, d), _BF16),
        pltpu.VMEM((ts, ATTN_WIDTH), _BF16),
        pltpu.VMEM((4, BLOCK + ts, LANES), _BF16),
        pltpu.VMEM((4, BLOCK + ts, LANES), _BF16),
        pltpu.VMEM((2 * PAIRS, ts, BLOCK), _F32),
        pltpu.VMEM((2 * PAIRS, ts, 2 * BLOCK), _BF16),
        pltpu.VMEM((ts, ATTN_WIDTH), _F32),
        pltpu.VMEM((ts, SGU_WIDTH), _F32),
        pltpu.VMEM((ts, SGU_WIDTH), _F32),
        pltpu.VMEM((ts, SGU_WIDTH), _BF16),
        pltpu.VMEM((ts, d), _F32),
        pltpu.VMEM((ts, d), _F32),
        pltpu.VMEM((ts, d), _BF16),
    ]
    return pl.pallas_call(
        kernel,
        grid=(n_tiles + 1,),
        in_specs=in_specs,
        out_specs=o_spec,
        out_shape=jax.ShapeDtypeStruct(x.shape, x.dtype),
        scratch_shapes=scratch,
        compiler_params=pltpu.CompilerParams(
            dimension_semantics=("arbitrary",),
            vmem_limit_bytes=VMEM_LIMIT_BYTES),
        name="hybrid_layer_in_ln" if apply_in_ln else "hybrid_layer",
    )(sinks, x, ln_in_g, ln_in_b, w_in, b_in, vn_g, vn_b, w_s, b_s, p_a, p_b, w_out, b_out,
      ln_g, ln_b)


def kernel(x, ln_in_g, ln_in_b, w_in, b_in, sinks, vn_g, vn_b, w_s, b_s, p_a, p_b, w_out,
           b_out, ln_g, ln_b):
    row = lambda a: a.reshape(1, -1)
    for l in range(DEPTH):
        bias = jnp.broadcast_to(b_s[l][:, :, None], (SGU_GROUPS, BLOCK, LANES))
        x = _layer(x, sinks[l], row(ln_in_g), row(ln_in_b),
                   w_in[l].astype(_BF16), row(b_in[l]), row(vn_g[l]), row(vn_b[l]),
                   w_s[l], bias, p_a[l].astype(_BF16), p_b[l].astype(_BF16),
                   w_out[l].astype(_BF16), row(b_out[l]), row(ln_g[l]), row(ln_b[l]),
                   apply_in_ln=(l == 0))
    return x
```

```python
import functools

import jax
import jax.numpy as jnp
from jax import lax
from jax.experimental import pallas as pl
from jax.experimental.pallas import tpu as pltpu

D_MODEL = 1024
DEPTH = 2
HEAD_DIM = 64
ATTN_WIDTH = 512
KV_WIDTH = 128
BLOCK = 128
SGU_WIDTH = 512
SGU_GROUPS = 4
ALPHA = (2.0 * DEPTH) ** 0.25
LN_EPS = 1e-5
LANES = 128
PAIRS = ATTN_WIDTH // LANES

_SPLITS = (ATTN_WIDTH, KV_WIDTH, KV_WIDTH, ATTN_WIDTH, SGU_WIDTH, SGU_WIDTH, SGU_WIDTH,
           D_MODEL, D_MODEL)
_OFFS = tuple(sum(_SPLITS[:i]) for i in range(len(_SPLITS) + 1))
IN_COLS = _OFFS[-1]
(_Q, _K, _V, _GA, _UB, _VB, _GB, _RA, _RB) = tuple(
    (_OFFS[i], _OFFS[i + 1]) for i in range(len(_SPLITS)))

SEQ_TILE = 256
VMEM_LIMIT_BYTES = 56 * 1024 * 1024

_F32 = jnp.float32
_BF16 = jnp.bfloat16
_NEG = float(jnp.finfo(jnp.float32).min)


def _layernorm(x, g, b):
    mu = jnp.mean(x, axis=-1, keepdims=True)
    xc = x - mu
    var = jnp.mean(xc * xc, axis=-1, keepdims=True)
    return xc * lax.rsqrt(var + LN_EPS) * g + b


def _sigmoid(x):
    return 0.5 * jnp.tanh(0.5 * x) + 0.5


def _silu(x):
    return x * _sigmoid(x)


def _gelu_tanh(x):
    c = 0.7978845608028654
    return 0.5 * x * (1.0 + jnp.tanh(c * (x + 0.044715 * (x * x * x))))


def _layer_kernel(sinks_ref, x_ref, lng_ref, lnb_ref, win_ref, bin_ref, vng_ref, vnb_ref,
                  ws_ref, bs_ref, pa_ref, pb_ref, wout_ref, bout_ref, g_ref, b_ref,
                  o_ref,
                  xn_s, xb_s, q_s, k_s, v_s, t_s, p_s, att_s, u_s, vb_s, vn_s, m_s, r_s, mb_s,
                  *, apply_in_ln, ts, tiles_per_seq, n_tiles):
    step = pl.program_id(0)
    first_tile = lax.rem(step, tiles_per_seq) == 0
    slot = lax.rem(step, 2)

    @pl.when(step == 0)
    def _():
        mb_s[...] = jnp.zeros(mb_s.shape, _BF16)
        xn_s[1] = jnp.zeros(xn_s.shape[1:], _F32)
        k_s[:, 0:BLOCK, :] = jnp.zeros((4, BLOCK, LANES), _BF16)
        v_s[:, 0:BLOCK, :] = jnp.zeros((4, BLOCK, LANES), _BF16)

    x = x_ref[0]
    if apply_in_ln:
        x = _layernorm(x, lng_ref[...], lnb_ref[...])
    xn_s[slot] = x
    xb_s[...] = x.astype(_BF16)

    def finish_previous(r0, rows):
        out = jnp.dot(mb_s[r0:r0 + rows, :], wout_ref[...], preferred_element_type=_F32)
        z = ALPHA * xn_s[1 - slot, r0:r0 + rows, :] + out + bout_ref[...]
        o_ref[0, r0:r0 + rows, :] = _layernorm(z, g_ref[...], b_ref[...])

    def proj(rng):
        lo, hi = rng
        return (jnp.dot(xb_s[...], win_ref[:, lo:hi], preferred_element_type=_F32)
                + bin_ref[:, lo:hi])

    lane = lax.broadcasted_iota(jnp.int32, (ts, LANES), 1)
    left = lane < HEAD_DIM

    def spread(t, dst):
        rot = pltpu.roll(t, HEAD_DIM, axis=1)
        zero = jnp.zeros_like(t)
        dst[0, BLOCK:BLOCK + ts, :] = jnp.where(left, t, zero).astype(_BF16)
        dst[1, BLOCK:BLOCK + ts, :] = jnp.where(left, zero, rot).astype(_BF16)
        dst[2, BLOCK:BLOCK + ts, :] = jnp.where(left, rot, zero).astype(_BF16)
        dst[3, BLOCK:BLOCK + ts, :] = jnp.where(left, zero, t).astype(_BF16)

    half = ts // 2
    finish_previous(0, half)
    spread(proj(_K), k_s)
    spread(proj(_V), v_s)
    finish_previous(half, half)
    q_s[...] = (proj(_Q) * (HEAD_DIM ** -0.5)).astype(_BF16)
    m_s[...] = _sigmoid(proj(_RA))

    pl.when(step < n_tiles)(functools.partial(
        _branches, sinks_ref, win_ref, bin_ref, vng_ref, vnb_ref, ws_ref, bs_ref, pa_ref,
        pb_ref, xb_s, q_s, k_s, v_s, t_s, p_s, att_s, u_s, vb_s, vn_s, m_s, r_s, mb_s,
        first_tile=first_tile, ts=ts))


def _branches(sinks_ref, win_ref, bin_ref, vng_ref, vnb_ref, ws_ref, bs_ref, pa_ref, pb_ref,
              xb_s, q_s, k_s, v_s, t_s, p_s, att_s, u_s, vb_s, vn_s, m_s, r_s, mb_s,
              *, first_tile, ts):
    nblk = ts // BLOCK

    def proj(rng):
        lo, hi = rng
        return (jnp.dot(xb_s[...], win_ref[:, lo:hi], preferred_element_type=_F32)
                + bin_ref[:, lo:hi])

    row = lax.broadcasted_iota(jnp.int32, (BLOCK, BLOCK), 0)
    col = lax.broadcasted_iota(jnp.int32, (BLOCK, BLOCK), 1)
    upper = col > row

    def scores():
        for j in range(nblk):
            r0 = j * BLOCK
            for p in range(PAIRS):
                q2 = q_s[r0:r0 + BLOCK, p * LANES:(p + 1) * LANES]
                for par in range(2):
                    kk = k_s[2 * (p // 2) + par, r0:r0 + 2 * BLOCK, :]
                    s = lax.dot_general(q2, kk, (((1,), (1,)), ((), ())),
                                        preferred_element_type=_F32)
                    t = jnp.where(upper, s[:, 0:BLOCK], s[:, BLOCK:2 * BLOCK])
                    if j == 0:
                        t = jnp.where(jnp.logical_and(upper, first_tile), _NEG, t)
                    t_s[2 * p + par, r0:r0 + BLOCK, :] = t

    trow = lax.broadcasted_iota(jnp.int32, (ts, BLOCK), 0) & (BLOCK - 1)
    tcol = lax.broadcasted_iota(jnp.int32, (ts, BLOCK), 1)
    upper_t = tcol > trow

    def softmax(hd):
        t = t_s[hd]
        sink = sinks_ref[hd]
        m = jnp.maximum(jnp.max(t, axis=-1, keepdims=True), sink)
        e = jnp.exp(t - m)
        den = jnp.sum(e, axis=-1, keepdims=True) + jnp.exp(sink - m)
        pn = e * (1.0 / den)
        zero = jnp.zeros_like(pn)
        p_s[hd, :, 0:BLOCK] = jnp.where(upper_t, pn, zero).astype(_BF16)
        p_s[hd, :, BLOCK:2 * BLOCK] = jnp.where(upper_t, zero, pn).astype(_BF16)

    def weighted_values():
        for j in range(nblk):
            r0 = j * BLOCK
            for p in range(PAIRS):
                acc = None
                for par in range(2):
                    vv = v_s[2 * (p // 2) + par, r0:r0 + 2 * BLOCK, :]
                    o = jnp.dot(p_s[2 * p + par, r0:r0 + BLOCK, :], vv,
                                preferred_element_type=_F32)
                    acc = o if acc is None else acc + o
                att_s[r0:r0 + BLOCK, p * LANES:(p + 1) * LANES] = acc

    def spatial_gate():
        tril = col <= row
        for g in range(SGU_GROUPS):
            w = jnp.where(tril, ws_ref[g], 0.0).astype(_BF16)
            bias = bs_ref[g]
            c0 = g * LANES
            for c in range(nblk):
                r0 = c * BLOCK
                mixed = jnp.dot(w, vn_s[r0:r0 + BLOCK, c0:c0 + LANES],
                                preferred_element_type=_F32) + bias
                u_s[r0:r0 + BLOCK, c0:c0 + LANES] = u_s[r0:r0 + BLOCK, c0:c0 + LANES] * mixed

    scores()
    vb_s[...] = _gelu_tanh(proj(_VB))
    for hd in range(PAIRS):
        softmax(hd)
    u_s[...] = _gelu_tanh(proj(_UB))
    for hd in range(PAIRS, 2 * PAIRS):
        softmax(hd)
    weighted_values()
    k_s[:, 0:BLOCK, :] = k_s[:, ts:ts + BLOCK, :]
    v_s[:, 0:BLOCK, :] = v_s[:, ts:ts + BLOCK, :]
    ga = _silu(proj(_GA))
    vn_s[...] = _layernorm(vb_s[...], vng_ref[...], vnb_ref[...]).astype(_BF16)
    ya = (att_s[...] * ga).astype(_BF16)
    pa = jnp.dot(ya, pa_ref[...], preferred_element_type=_F32)
    m_s[...] = m_s[...] * pa
    r_s[...] = _sigmoid(proj(_RB))
    spatial_gate()
    yb = (u_s[...] * _silu(proj(_GB))).astype(_BF16)
    pb = jnp.dot(yb, pb_ref[...], preferred_element_type=_F32)
    mb_s[...] = (m_s[...] + r_s[...] * pb).astype(_BF16)


def _resident(shape):
    nd = len(shape)
    return pl.BlockSpec(shape, lambda i: (0,) * nd, pipeline_mode=pl.Buffered(1))


def _layer(x, sinks, ln_in_g, ln_in_b, w_in, b_in, vn_g, vn_b, w_s, b_s, p_a, p_b, w_out,
           b_out, ln_g, ln_b, *, apply_in_ln):
    batch, seq, d = x.shape
    ts = SEQ_TILE
    assert d == D_MODEL and seq % ts == 0 and ts % BLOCK == 0
    tiles_per_seq = seq // ts
    n_tiles = batch * tiles_per_seq
    kernel = functools.partial(_layer_kernel, apply_in_ln=apply_in_ln, ts=ts,
                               tiles_per_seq=tiles_per_seq, n_tiles=n_tiles)

    def tile_index(t):
        return (t // tiles_per_seq, t % tiles_per_seq, 0)

    x_spec = pl.BlockSpec((1, ts, d), lambda i: tile_index(jnp.minimum(i, n_tiles - 1)))
    o_spec = pl.BlockSpec((1, ts, d), lambda i: tile_index(jnp.maximum(i - 1, 0)))
    in_specs = [
        pl.BlockSpec(memory_space=pltpu.SMEM),
        x_spec,
        _resident((1, d)), _resident((1, d)),
        _resident((d, IN_COLS)), _resident((1, IN_COLS)),
        _resident((1, SGU_WIDTH)), _resident((1, SGU_WIDTH)),
        _resident((SGU_GROUPS, BLOCK, BLOCK)), _resident((SGU_GROUPS, BLOCK, LANES)),
        _resident((ATTN_WIDTH, d)), _resident((SGU_WIDTH, d)),
        _resident((d, d)), _resident((1, d)),
        _resident((1, d)), _resident((1, d)),
    ]
    scratch = [
        pltpu.VMEM((2, ts, d), _F32),
        pltpu.VMEM((ts, d), _BF16),
        pltpu.VMEM((ts, ATTN_WIDTH), _BF16),
        pltpu.VMEM((4, BLOCK + ts, LANES), _BF16),
        pltpu.VMEM((4, BLOCK + ts, LANES), _BF16),
        pltpu.VMEM((2 * PAIRS, ts, BLOCK), _F32),
        pltpu.VMEM((2 * PAIRS, ts, 2 * BLOCK), _BF16),
        pltpu.VMEM((ts, ATTN_WIDTH), _F32),
        pltpu.VMEM((ts, SGU_WIDTH), _F32),
        pltpu.VMEM((ts, SGU_WIDTH), _F32),
        pltpu.VMEM((ts, SGU_WIDTH), _BF16),
        pltpu.VMEM((ts, d), _F32),
        pltpu.VMEM((ts, d), _F32),
        pltpu.VMEM((ts, d), _BF16),
    ]
    return pl.pallas_call(
        kernel,
        grid=(n_tiles + 1,),
        in_specs=in_specs,
        out_specs=o_spec,
        out_shape=jax.ShapeDtypeStruct(x.shape, x.dtype),
        scratch_shapes=scratch,
        compiler_params=pltpu.CompilerParams(
            dimension_semantics=("arbitrary",),
            vmem_limit_bytes=VMEM_LIMIT_BYTES),
        name="hybrid_layer_in_ln" if apply_in_ln else "hybrid_layer",
    )(sinks, x, ln_in_g, ln_in_b, w_in, b_in, vn_g, vn_b, w_s, b_s, p_a, p_b, w_out, b_out,
      ln_g, ln_b)


def kernel(x, ln_in_g, ln_in_b, w_in, b_in, sinks, vn_g, vn_b, w_s, b_s, p_a, p_b, w_out,
           b_out, ln_g, ln_b):
    row = lambda a: a.reshape(1, -1)
    for l in range(DEPTH):
        bias = jnp.broadcast_to(b_s[l][:, :, None], (SGU_GROUPS, BLOCK, LANES))
        x = _layer(x, sinks[l], row(ln_in_g), row(ln_in_b),
                   w_in[l].astype(_BF16), row(b_in[l]), row(vn_g[l]), row(vn_b[l]),
                   w_s[l], bias, p_a[l].astype(_BF16), p_b[l].astype(_BF16),
                   w_out[l].astype(_BF16), row(b_out[l]), row(ln_g[l]), row(ln_b[l]),
                   apply_in_ln=(l == 0))
    return x
```

```python
import functools

import jax
import jax.numpy as jnp
from jax import lax
from jax.experimental import pallas as pl
from jax.experimental.pallas import tpu as pltpu

D_MODEL = 1024
DEPTH = 2
HEAD_DIM = 64
ATTN_WIDTH = 512
KV_WIDTH = 128
BLOCK = 128
SGU_WIDTH = 512
SGU_GROUPS = 4
ALPHA = (2.0 * DEPTH) ** 0.25
LN_EPS = 1e-5
LANES = 128
PAIRS = ATTN_WIDTH // LANES

_SPLITS = (ATTN_WIDTH, KV_WIDTH, KV_WIDTH, ATTN_WIDTH, SGU_WIDTH, SGU_WIDTH, SGU_WIDTH,
           D_MODEL, D_MODEL)
_OFFS = tuple(sum(_SPLITS[:i]) for i in range(len(_SPLITS) + 1))
IN_COLS = _OFFS[-1]
(_Q, _K, _V, _GA, _UB, _VB, _GB, _RA, _RB) = tuple(
    (_OFFS[i], _OFFS[i + 1]) for i in range(len(_SPLITS)))
_COL_SCALE = (HEAD_DIM ** -0.5, 1.0, 1.0, 0.5, 1.0, 1.0, 0.5, 0.5, 0.5)

SEQ_TILE = 512
VMEM_LIMIT_BYTES = 56 * 1024 * 1024
WEIGHT_COL_PAD = LANES

_F32 = jnp.float32
_BF16 = jnp.bfloat16
_NEG = float(jnp.finfo(jnp.float32).min)


def _layernorm(x, g, b):
    mu = jnp.mean(x, axis=-1, keepdims=True)
    xc = x - mu
    var = jnp.mean(xc * xc, axis=-1, keepdims=True)
    return xc * lax.rsqrt(var + LN_EPS) * g + b


def _silu_of_double(h):
    return h * jnp.tanh(h) + h


def _gelu_tanh(x):
    c = 0.7978845608028654
    hx = 0.5 * x
    return hx * jnp.tanh(x * ((c * 0.044715) * (x * x) + c)) + hx


def _layer_kernel(sinks_ref, x_ref, lng_ref, lnb_ref, win_ref, bin_ref, vng_ref, vnb_ref,
                  ws_ref, bs_ref, pa_ref, pb_ref, wout_ref, bout_ref, g_ref, b_ref,
                  o_ref,
                  res_s, xb_s, q_s, k_s, v_s, t_s, p_s, att_s, u_s, vb_s, vn_s, m_s, mb_s,
                  *, apply_in_ln, ts):
    nblk = ts // BLOCK
    first_tile = pl.program_id(1) == 0

    x = x_ref[0]
    if apply_in_ln:
        x = _layernorm(x, lng_ref[...], lnb_ref[...])
    res_s[...] = ALPHA * x + bout_ref[...]
    xb_s[...] = x.astype(_BF16)

    def proj(rng):
        lo, hi = rng
        return (jnp.dot(xb_s[...], win_ref[:, lo:hi], preferred_element_type=_F32)
                + bin_ref[:, lo:hi])

    lane = lax.broadcasted_iota(jnp.int32, (ts, LANES), 1)
    left = lane < HEAD_DIM

    q_s[...] = proj(_Q).astype(_BF16)

    @pl.when(first_tile)
    def _():
        k_s[:, 0:BLOCK, :] = jnp.zeros((4, BLOCK, LANES), _BF16)
        v_s[:, 0:BLOCK, :] = jnp.zeros((4, BLOCK, LANES), _BF16)

    @pl.when(jnp.logical_not(first_tile))
    def _():
        k_s[:, 0:BLOCK, :] = k_s[:, ts:ts + BLOCK, :]
        v_s[:, 0:BLOCK, :] = v_s[:, ts:ts + BLOCK, :]

    def spread(t, dst):
        rot = pltpu.roll(t, HEAD_DIM, axis=1)
        zero = jnp.zeros_like(t)
        dst[0, BLOCK:BLOCK + ts, :] = jnp.where(left, t, zero).astype(_BF16)
        dst[1, BLOCK:BLOCK + ts, :] = jnp.where(left, zero, rot).astype(_BF16)
        dst[2, BLOCK:BLOCK + ts, :] = jnp.where(left, rot, zero).astype(_BF16)
        dst[3, BLOCK:BLOCK + ts, :] = jnp.where(left, zero, t).astype(_BF16)

    spread(proj(_K), k_s)
    spread(proj(_V), v_s)

    row = lax.broadcasted_iota(jnp.int32, (BLOCK, BLOCK), 0)
    col = lax.broadcasted_iota(jnp.int32, (BLOCK, BLOCK), 1)
    upper = col > row

    def scores():
        for j in range(nblk):
            r0 = j * BLOCK
            for p in range(PAIRS):
                q2 = q_s[r0:r0 + BLOCK, p * LANES:(p + 1) * LANES]
                for par in range(2):
                    kk = k_s[2 * (p // 2) + par, r0:r0 + 2 * BLOCK, :]
                    s = lax.dot_general(q2, kk, (((1,), (1,)), ((), ())),
                                        preferred_element_type=_F32)
                    t = jnp.where(upper, s[:, 0:BLOCK], s[:, BLOCK:2 * BLOCK])
                    if j == 0:
                        t = jnp.where(jnp.logical_and(upper, first_tile), _NEG, t)
                    t_s[2 * p + par, r0:r0 + BLOCK, :] = t

    trow = lax.broadcasted_iota(jnp.int32, (ts, BLOCK), 0) & (BLOCK - 1)
    tcol = lax.broadcasted_iota(jnp.int32, (ts, BLOCK), 1)
    upper_t = tcol > trow

    def softmax(hd):
        t = t_s[hd]
        sink = sinks_ref[hd]
        m = jnp.maximum(jnp.max(t, axis=-1, keepdims=True), sink)
        e = jnp.exp(t - m)
        den = jnp.sum(e, axis=-1, keepdims=True) + jnp.exp(sink - m)
        pn = e * (1.0 / den)
        zero = jnp.zeros_like(pn)
        p_s[hd, :, 0:BLOCK] = jnp.where(upper_t, pn, zero).astype(_BF16)
        p_s[hd, :, BLOCK:2 * BLOCK] = jnp.where(upper_t, zero, pn).astype(_BF16)

    def weighted_values():
        for j in range(nblk):
            r0 = j * BLOCK
            for p in range(PAIRS):
                acc = None
                for par in range(2):
                    vv = v_s[2 * (p // 2) + par, r0:r0 + 2 * BLOCK, :]
                    o = jnp.dot(p_s[2 * p + par, r0:r0 + BLOCK, :], vv,
                                preferred_element_type=_F32)
                    acc = o if acc is None else acc + o
                att_s[r0:r0 + BLOCK, p * LANES:(p + 1) * LANES] = acc

    def spatial_gate():
        tril = col <= row
        for g in range(SGU_GROUPS):
            w = jnp.where(tril, ws_ref[g], 0.0).astype(_BF16)
            bias = bs_ref[g]
            c0 = g * LANES
            for c in range(nblk):
                r0 = c * BLOCK
                mixed = jnp.dot(w, vn_s[r0:r0 + BLOCK, c0:c0 + LANES],
                                preferred_element_type=_F32) + bias
                u_s[r0:r0 + BLOCK, c0:c0 + LANES] = u_s[r0:r0 + BLOCK, c0:c0 + LANES] * mixed

    scores()
    vb_s[...] = _gelu_tanh(proj(_VB))
    for hd in range(PAIRS):
        softmax(hd)
    u_s[...] = _gelu_tanh(proj(_UB))
    for hd in range(PAIRS, 2 * PAIRS):
        softmax(hd)
    weighted_values()
    ga = _silu_of_double(proj(_GA))
    vn_s[...] = _layernorm(vb_s[...], vng_ref[...], vnb_ref[...]).astype(_BF16)
    ya = (att_s[...] * ga).astype(_BF16)
    pa = jnp.dot(ya, pa_ref[:, 0:D_MODEL], preferred_element_type=_F32)
    m_s[...] = (jnp.tanh(proj(_RA)) + 1.0) * pa
    spatial_gate()
    yb = (u_s[...] * _silu_of_double(proj(_GB))).astype(_BF16)
    pb = jnp.dot(yb, pb_ref[:, 0:D_MODEL], preferred_element_type=_F32)
    mb_s[...] = (m_s[...] + (jnp.tanh(proj(_RB)) + 1.0) * pb).astype(_BF16)

    half = ts // 2
    for r0 in (0, half):
        out = jnp.dot(mb_s[r0:r0 + half, :], wout_ref[:, 0:D_MODEL],
                      preferred_element_type=_F32)
        o_ref[0, r0:r0 + half, :] = _layernorm(res_s[r0:r0 + half, :] + out,
                                               g_ref[...], b_ref[...])


def _resident(shape):
    nd = len(shape)
    return pl.BlockSpec(shape, lambda b, s: (0,) * nd, pipeline_mode=pl.Buffered(1))


def _layer(x, sinks, ln_in_g, ln_in_b, w_in, b_in, vn_g, vn_b, w_s, b_s, p_a, p_b, w_out,
           b_out, ln_g, ln_b, *, apply_in_ln):
    batch, seq, d = x.shape
    ts = SEQ_TILE
    assert d == D_MODEL and seq % ts == 0 and ts % BLOCK == 0
    kernel = functools.partial(_layer_kernel, apply_in_ln=apply_in_ln, ts=ts)
    tile = pl.BlockSpec((1, ts, d), lambda b, s: (b, s, 0))
    dp = d + WEIGHT_COL_PAD
    in_specs = [
        pl.BlockSpec(memory_space=pltpu.SMEM),
        tile,
        _resident((1, d)), _resident((1, d)),
        _resident((d, IN_COLS)), _resident((1, IN_COLS)),
        _resident((1, SGU_WIDTH)), _resident((1, SGU_WIDTH)),
        _resident((SGU_GROUPS, BLOCK, BLOCK)), _resident((SGU_GROUPS, BLOCK, LANES)),
        _resident((ATTN_WIDTH, dp)), _resident((SGU_WIDTH, dp)),
        _resident((d, dp)), _resident((1, d)),
        _resident((1, d)), _resident((1, d)),
    ]
    scratch = [
        pltpu.VMEM((ts, d), _F32),
        pltpu.VMEM((ts, d), _BF16),
        pltpu.VMEM((ts, ATTN_WIDTH), _BF16),
        pltpu.VMEM((4, BLOCK + ts, LANES), _BF16),
        pltpu.VMEM((4, BLOCK + ts, LANES), _BF16),
        pltpu.VMEM((2 * PAIRS, ts, BLOCK), _F32),
        pltpu.VMEM((2 * PAIRS, ts, 2 * BLOCK), _BF16),
        pltpu.VMEM((ts, ATTN_WIDTH), _F32),
        pltpu.VMEM((ts, SGU_WIDTH), _F32),
        pltpu.VMEM((ts, SGU_WIDTH), _F32),
        pltpu.VMEM((ts, SGU_WIDTH), _BF16),
        pltpu.VMEM((ts, d), _F32),
        pltpu.VMEM((ts, d), _BF16),
    ]
    return pl.pallas_call(
        kernel,
        grid=(batch, seq // ts),
        in_specs=in_specs,
        out_specs=tile,
        out_shape=jax.ShapeDtypeStruct(x.shape, x.dtype),
        scratch_shapes=scratch,
        compiler_params=pltpu.CompilerParams(
            dimension_semantics=("arbitrary", "arbitrary"),
            vmem_limit_bytes=VMEM_LIMIT_BYTES),
        name="hybrid_layer_in_ln" if apply_in_ln else "hybrid_layer",
    )(sinks, x, ln_in_g, ln_in_b, w_in, b_in, vn_g, vn_b, w_s, b_s, p_a, p_b, w_out, b_out,
      ln_g, ln_b)


def kernel(x, ln_in_g, ln_in_b, w_in, b_in, sinks, vn_g, vn_b, w_s, b_s, p_a, p_b, w_out,
           b_out, ln_g, ln_b):
    row = lambda a: a.reshape(1, -1)
    pad_cols = lambda w: jnp.pad(w.astype(_BF16), ((0, 0), (0, WEIGHT_COL_PAD)))
    col_scale = jnp.concatenate(
        [jnp.full((n,), s, _F32) for n, s in zip(_SPLITS, _COL_SCALE)])
    for l in range(DEPTH):
        bias = jnp.broadcast_to(b_s[l][:, :, None], (SGU_GROUPS, BLOCK, LANES))
        x = _layer(x, sinks[l], row(ln_in_g), row(ln_in_b),
                   (w_in[l] * col_scale).astype(_BF16), row(b_in[l] * col_scale),
                   row(vn_g[l]), row(vn_b[l]), w_s[l], bias,
                   pad_cols(0.5 * p_a[l]), pad_cols(0.5 * p_b[l]),
                   pad_cols(w_out[l]), row(b_out[l]), row(ln_g[l]), row(ln_b[l]),
                   apply_in_ln=(l == 0))
    return x
```

```python
import functools

import jax
import jax.numpy as jnp
from jax import lax
from jax.experimental import pallas as pl
from jax.experimental.pallas import tpu as pltpu

D_MODEL = 1024
DEPTH = 2
HEAD_DIM = 64
ATTN_WIDTH = 512
KV_WIDTH = 128
BLOCK = 128
SGU_WIDTH = 512
SGU_GROUPS = 4
ALPHA = (2.0 * DEPTH) ** 0.25
LN_EPS = 1e-5
LOG2E = 1.4426950408889634
LANES = 128
PAIRS = ATTN_WIDTH // LANES

_SPLITS = (ATTN_WIDTH, KV_WIDTH, KV_WIDTH, ATTN_WIDTH, SGU_WIDTH, SGU_WIDTH, SGU_WIDTH,
           D_MODEL, D_MODEL)
_OFFS = tuple(sum(_SPLITS[:i]) for i in range(len(_SPLITS) + 1))
IN_COLS = _OFFS[-1]
(_Q, _K, _V, _GA, _UB, _VB, _GB, _RA, _RB) = tuple(
    (_OFFS[i], _OFFS[i + 1]) for i in range(len(_SPLITS)))
_COL_SCALE = (HEAD_DIM ** -0.5 * LOG2E, 1.0, 1.0, 0.5, 1.0, 1.0, 0.5, 0.5, 0.5)

SEQ_TILE = 512
VMEM_LIMIT_BYTES = 56 * 1024 * 1024
WEIGHT_COL_PAD = LANES

_F32 = jnp.float32
_BF16 = jnp.bfloat16
_NEG = float(jnp.finfo(jnp.float32).min)


def _layernorm(x, g, b):
    mu = jnp.mean(x, axis=-1, keepdims=True)
    xc = x - mu
    var = jnp.mean(xc * xc, axis=-1, keepdims=True)
    return xc * lax.rsqrt(var + LN_EPS) * g + b


def _silu_of_double(h):
    return h * jnp.tanh(h) + h


def _gelu_tanh(x):
    c = 0.7978845608028654
    hx = 0.5 * x
    return hx * jnp.tanh(x * ((c * 0.044715) * (x * x) + c)) + hx


def _layer_kernel(sinks_ref, x_ref, lng_ref, lnb_ref, win_ref, bin_ref, vng_ref, vnb_ref,
                  ws_ref, bs_ref, pa_ref, pb_ref, wout_ref, bout_ref, g_ref, b_ref,
                  o_ref,
                  res_s, xb_s, q_s, k_s, v_s, t_s, p_s, att_s, u_s, vb_s, vn_s, m_s, mb_s,
                  *, apply_in_ln, ts):
    nblk = ts // BLOCK
    first_tile = pl.program_id(1) == 0

    x = x_ref[0]
    if apply_in_ln:
        x = _layernorm(x, lng_ref[...], lnb_ref[...])
    res_s[...] = ALPHA * x + bout_ref[...]
    xb_s[...] = x.astype(_BF16)

    def proj(rng):
        lo, hi = rng
        return (jnp.dot(xb_s[...], win_ref[:, lo:hi], preferred_element_type=_F32)
                + bin_ref[:, lo:hi])

    lane = lax.broadcasted_iota(jnp.int32, (ts, LANES), 1)
    left = lane < HEAD_DIM

    q_s[...] = proj(_Q).astype(_BF16)

    @pl.when(first_tile)
    def _():
        k_s[:, 0:BLOCK, :] = jnp.zeros((4, BLOCK, LANES), _BF16)
        v_s[:, 0:BLOCK, :] = jnp.zeros((4, BLOCK, LANES), _BF16)

    @pl.when(jnp.logical_not(first_tile))
    def _():
        k_s[:, 0:BLOCK, :] = k_s[:, ts:ts + BLOCK, :]
        v_s[:, 0:BLOCK, :] = v_s[:, ts:ts + BLOCK, :]

    def spread(t, dst):
        rot = pltpu.roll(t, HEAD_DIM, axis=1)
        zero = jnp.zeros_like(t)
        dst[0, BLOCK:BLOCK + ts, :] = jnp.where(left, t, zero).astype(_BF16)
        dst[1, BLOCK:BLOCK + ts, :] = jnp.where(left, zero, rot).astype(_BF16)
        dst[2, BLOCK:BLOCK + ts, :] = jnp.where(left, rot, zero).astype(_BF16)
        dst[3, BLOCK:BLOCK + ts, :] = jnp.where(left, zero, t).astype(_BF16)

    spread(proj(_K), k_s)
    spread(proj(_V), v_s)

    row = lax.broadcasted_iota(jnp.int32, (BLOCK, BLOCK), 0)
    col = lax.broadcasted_iota(jnp.int32, (BLOCK, BLOCK), 1)
    upper = col > row

    def scores():
        for j in range(nblk):
            r0 = j * BLOCK
            for p in range(PAIRS):
                q2 = q_s[r0:r0 + BLOCK, p * LANES:(p + 1) * LANES]
                for par in range(2):
                    kk = k_s[2 * (p // 2) + par, r0:r0 + 2 * BLOCK, :]
                    s = lax.dot_general(q2, kk, (((1,), (1,)), ((), ())),
                                        preferred_element_type=_F32)
                    t = jnp.where(upper, s[:, 0:BLOCK], s[:, BLOCK:2 * BLOCK])
                    if j == 0:
                        t = jnp.where(jnp.logical_and(upper, first_tile), _NEG, t)
                    t_s[2 * p + par, r0:r0 + BLOCK, :] = t

    trow = lax.broadcasted_iota(jnp.int32, (ts, BLOCK), 0) & (BLOCK - 1)
    tcol = lax.broadcasted_iota(jnp.int32, (ts, BLOCK), 1)
    upper_t = tcol > trow

    def softmax(hd):
        t = t_s[hd]
        sink = sinks_ref[hd] * LOG2E
        m = jnp.maximum(jnp.max(t, axis=-1, keepdims=True), sink)
        e = jnp.exp2(t - m)
        den = jnp.sum(e, axis=-1, keepdims=True) + jnp.exp2(sink - m)
        pn = e * (1.0 / den)
        zero = jnp.zeros_like(pn)
        p_s[hd, :, 0:BLOCK] = jnp.where(upper_t, pn, zero).astype(_BF16)
        p_s[hd, :, BLOCK:2 * BLOCK] = jnp.where(upper_t, zero, pn).astype(_BF16)

    def weighted_values():
        for j in range(nblk):
            r0 = j * BLOCK
            for p in range(PAIRS):
                acc = None
                for par in range(2):
                    vv = v_s[2 * (p // 2) + par, r0:r0 + 2 * BLOCK, :]
                    o = jnp.dot(p_s[2 * p + par, r0:r0 + BLOCK, :], vv,
                                preferred_element_type=_F32)
                    acc = o if acc is None else acc + o
                att_s[r0:r0 + BLOCK, p * LANES:(p + 1) * LANES] = acc

    def spatial_gate():
        tril = col <= row
        for g in range(SGU_GROUPS):
            w = jnp.where(tril, ws_ref[g], 0.0).astype(_BF16)
            bias = bs_ref[g]
            c0 = g * LANES
            mixed = jnp.dot(w, vn_s[:, g * nblk * LANES:(g + 1) * nblk * LANES],
                            preferred_element_type=_F32)
            for c in range(nblk):
                r0 = c * BLOCK
                u_s[r0:r0 + BLOCK, c0:c0 + LANES] = u_s[r0:r0 + BLOCK, c0:c0 + LANES] * (
                    mixed[:, c * LANES:(c + 1) * LANES] + bias)

    scores()
    vb_s[...] = _gelu_tanh(proj(_VB))
    for hd in range(PAIRS):
        softmax(hd)
    u_s[...] = _gelu_tanh(proj(_UB))
    for hd in range(PAIRS, 2 * PAIRS):
        softmax(hd)
    weighted_values()
    ga = _silu_of_double(proj(_GA))
    vn = _layernorm(vb_s[...], vng_ref[...], vnb_ref[...]).astype(_BF16)
    for g in range(SGU_GROUPS):
        for c in range(nblk):
            vn_s[:, (g * nblk + c) * LANES:(g * nblk + c + 1) * LANES] = (
                vn[c * BLOCK:(c + 1) * BLOCK, g * LANES:(g + 1) * LANES])
    ya = (att_s[...] * ga).astype(_BF16)
    pa = jnp.dot(ya, pa_ref[:, 0:D_MODEL], preferred_element_type=_F32)
    m_s[...] = (jnp.tanh(proj(_RA)) + 1.0) * pa
    spatial_gate()
    yb = (u_s[...] * _silu_of_double(proj(_GB))).astype(_BF16)
    pb = jnp.dot(yb, pb_ref[:, 0:D_MODEL], preferred_element_type=_F32)
    mb_s[...] = (m_s[...] + (jnp.tanh(proj(_RB)) + 1.0) * pb).astype(_BF16)

    half = ts // 2
    for r0 in (0, half):
        out = jnp.dot(mb_s[r0:r0 + half, :], wout_ref[:, 0:D_MODEL],
                      preferred_element_type=_F32)
        o_ref[0, r0:r0 + half, :] = _layernorm(res_s[r0:r0 + half, :] + out,
                                               g_ref[...], b_ref[...])


def _resident(shape):
    nd = len(shape)
    return pl.BlockSpec(shape, lambda b, s: (0,) * nd, pipeline_mode=pl.Buffered(1))


def _layer(x, sinks, ln_in_g, ln_in_b, w_in, b_in, vn_g, vn_b, w_s, b_s, p_a, p_b, w_out,
           b_out, ln_g, ln_b, *, apply_in_ln):
    batch, seq, d = x.shape
    ts = SEQ_TILE
    assert d == D_MODEL and seq % ts == 0 and ts % BLOCK == 0
    kernel = functools.partial(_layer_kernel, apply_in_ln=apply_in_ln, ts=ts)
    tile = pl.BlockSpec((1, ts, d), lambda b, s: (b, s, 0))
    dp = d + WEIGHT_COL_PAD
    in_specs = [
        pl.BlockSpec(memory_space=pltpu.SMEM),
        tile,
        _resident((1, d)), _resident((1, d)),
        _resident((d, IN_COLS)), _resident((1, IN_COLS)),
        _resident((1, SGU_WIDTH)), _resident((1, SGU_WIDTH)),
        _resident((SGU_GROUPS, BLOCK, BLOCK)), _resident((SGU_GROUPS, BLOCK, LANES)),
        _resident((ATTN_WIDTH, dp)), _resident((SGU_WIDTH, dp)),
        _resident((d, dp)), _resident((1, d)),
        _resident((1, d)), _resident((1, d)),
    ]
    scratch = [
        pltpu.VMEM((ts, d), _F32),
        pltpu.VMEM((ts, d), _BF16),
        pltpu.VMEM((ts, ATTN_WIDTH), _BF16),
        pltpu.VMEM((4, BLOCK + ts, LANES), _BF16),
        pltpu.VMEM((4, BLOCK + ts, LANES), _BF16),
        pltpu.VMEM((2 * PAIRS, ts, BLOCK), _F32),
        pltpu.VMEM((2 * PAIRS, ts, 2 * BLOCK), _BF16),
        pltpu.VMEM((ts, ATTN_WIDTH), _F32),
        pltpu.VMEM((ts, SGU_WIDTH), _F32),
        pltpu.VMEM((ts, SGU_WIDTH), _F32),
        pltpu.VMEM((BLOCK, (ts // BLOCK) * SGU_WIDTH), _BF16),
        pltpu.VMEM((ts, d), _F32),
        pltpu.VMEM((ts, d), _BF16),
    ]
    return pl.pallas_call(
        kernel,
        grid=(batch, seq // ts),
        in_specs=in_specs,
        out_specs=tile,
        out_shape=jax.ShapeDtypeStruct(x.shape, x.dtype),
        scratch_shapes=scratch,
        compiler_params=pltpu.CompilerParams(
            dimension_semantics=("arbitrary", "arbitrary"),
            vmem_limit_bytes=VMEM_LIMIT_BYTES),
        name="hybrid_layer_in_ln" if apply_in_ln else "hybrid_layer",
    )(sinks, x, ln_in_g, ln_in_b, w_in, b_in, vn_g, vn_b, w_s, b_s, p_a, p_b, w_out, b_out,
      ln_g, ln_b)


def kernel(x, ln_in_g, ln_in_b, w_in, b_in, sinks, vn_g, vn_b, w_s, b_s, p_a, p_b, w_out,
           b_out, ln_g, ln_b):
    row = lambda a: a.reshape(1, -1)
    pad_cols = lambda w: jnp.pad(w.astype(_BF16), ((0, 0), (0, WEIGHT_COL_PAD)))
    col_scale = jnp.concatenate(
        [jnp.full((n,), s, _F32) for n, s in zip(_SPLITS, _COL_SCALE)])
    for l in range(DEPTH):
        bias = jnp.broadcast_to(b_s[l][:, :, None], (SGU_GROUPS, BLOCK, LANES))
        x = _layer(x, sinks[l], row(ln_in_g), row(ln_in_b),
                   (w_in[l] * col_scale).astype(_BF16), row(b_in[l] * col_scale),
                   row(vn_g[l]), row(vn_b[l]), w_s[l], bias,
                   pad_cols(0.5 * p_a[l]), pad_cols(0.5 * p_b[l]),
                   pad_cols(w_out[l]), row(b_out[l]), row(ln_g[l]), row(ln_b[l]),
                   apply_in_ln=(l == 0))
    return x
```

```python
import functools

import jax
import jax.numpy as jnp
from jax import lax
from jax.experimental import pallas as pl
from jax.experimental.pallas import tpu as pltpu

D_MODEL = 1024
DEPTH = 2
HEAD_DIM = 64
ATTN_WIDTH = 512
KV_WIDTH = 128
BLOCK = 128
SGU_WIDTH = 512
SGU_GROUPS = 4
ALPHA = (2.0 * DEPTH) ** 0.25
LN_EPS = 1e-5
LOG2E = 1.4426950408889634
LANES = 128
PAIRS = ATTN_WIDTH // LANES

_SPLITS = (ATTN_WIDTH, KV_WIDTH, KV_WIDTH, ATTN_WIDTH, SGU_WIDTH, SGU_WIDTH, SGU_WIDTH,
           D_MODEL, D_MODEL)
_OFFS = tuple(sum(_SPLITS[:i]) for i in range(len(_SPLITS) + 1))
IN_COLS = _OFFS[-1]
(_Q, _K, _V, _GA, _UB, _VB, _GB, _RA, _RB) = tuple(
    (_OFFS[i], _OFFS[i + 1]) for i in range(len(_SPLITS)))
_COL_SCALE = (HEAD_DIM ** -0.5 * LOG2E, 1.0, 1.0, 0.5, 1.0, 1.0, 0.5, 0.5, 0.5)

SEQ_TILE = 512
VMEM_LIMIT_BYTES = 56 * 1024 * 1024
WEIGHT_COL_PAD = LANES

_F32 = jnp.float32
_BF16 = jnp.bfloat16
_NEG = float(jnp.finfo(jnp.float32).min)


def _layernorm(x, g, b):
    mu = jnp.mean(x, axis=-1, keepdims=True)
    xc = x - mu
    var = jnp.mean(xc * xc, axis=-1, keepdims=True)
    return xc * lax.rsqrt(var + LN_EPS) * g + b


def _silu_of_double(h):
    return h * jnp.tanh(h) + h


def _gelu_tanh(x):
    c = 0.7978845608028654
    hx = 0.5 * x
    return hx * jnp.tanh(x * ((c * 0.044715) * (x * x) + c)) + hx


def _layer_kernel(sinks_ref, x_ref, lng_ref, lnb_ref, win_ref, bin_ref, vng_ref, vnb_ref,
                  ws_ref, bs_ref, pa_ref, pb_ref, wout_ref, bout_ref, g_ref, b_ref,
                  o_ref,
                  res_s, xb_s, q_s, k_s, v_s, t_s, p_s, att_s, u_s, vb_s, vn_s, m_s, mb_s,
                  *, apply_in_ln, ts):
    nblk = ts // BLOCK
    first_tile = pl.program_id(1) == 0

    x = x_ref[0]
    if apply_in_ln:
        x = _layernorm(x, lng_ref[...], lnb_ref[...])
    res_s[...] = ALPHA * x + bout_ref[...]
    xb_s[...] = x.astype(_BF16)

    def proj(rng):
        lo, hi = rng
        return (jnp.dot(xb_s[...], win_ref[:, lo:hi], preferred_element_type=_F32)
                + bin_ref[:, lo:hi])

    lane = lax.broadcasted_iota(jnp.int32, (ts, LANES), 1)
    left = lane < HEAD_DIM

    q_s[...] = proj(_Q).astype(_BF16)

    @pl.when(first_tile)
    def _():
        k_s[:, 0:BLOCK, :] = jnp.zeros((4, BLOCK, LANES), _BF16)
        v_s[:, 0:BLOCK, :] = jnp.zeros((4, BLOCK, LANES), _BF16)

    @pl.when(jnp.logical_not(first_tile))
    def _():
        k_s[:, 0:BLOCK, :] = k_s[:, ts:ts + BLOCK, :]
        v_s[:, 0:BLOCK, :] = v_s[:, ts:ts + BLOCK, :]

    def spread(t, dst):
        rot = pltpu.roll(t, HEAD_DIM, axis=1)
        zero = jnp.zeros_like(t)
        dst[0, BLOCK:BLOCK + ts, :] = jnp.where(left, t, zero).astype(_BF16)
        dst[1, BLOCK:BLOCK + ts, :] = jnp.where(left, zero, rot).astype(_BF16)
        dst[2, BLOCK:BLOCK + ts, :] = jnp.where(left, rot, zero).astype(_BF16)
        dst[3, BLOCK:BLOCK + ts, :] = jnp.where(left, zero, t).astype(_BF16)

    kv = proj((_K[0], _V[1]))
    spread(kv[:, 0:KV_WIDTH], k_s)
    spread(kv[:, KV_WIDTH:2 * KV_WIDTH], v_s)

    row = lax.broadcasted_iota(jnp.int32, (BLOCK, BLOCK), 0)
    col = lax.broadcasted_iota(jnp.int32, (BLOCK, BLOCK), 1)
    upper = col > row

    def scores():
        for j in range(nblk):
            r0 = j * BLOCK
            for p in range(PAIRS):
                q2 = q_s[r0:r0 + BLOCK, p * LANES:(p + 1) * LANES]
                for par in range(2):
                    kk = k_s[2 * (p // 2) + par, r0:r0 + 2 * BLOCK, :]
                    s = lax.dot_general(q2, kk, (((1,), (1,)), ((), ())),
                                        preferred_element_type=_F32)
                    t = jnp.where(upper, s[:, 0:BLOCK], s[:, BLOCK:2 * BLOCK])
                    if j == 0:
                        t = jnp.where(jnp.logical_and(upper, first_tile), _NEG, t)
                    t_s[2 * p + par, r0:r0 + BLOCK, :] = t

    trow = lax.broadcasted_iota(jnp.int32, (ts, BLOCK), 0) & (BLOCK - 1)
    tcol = lax.broadcasted_iota(jnp.int32, (ts, BLOCK), 1)
    upper_t = tcol > trow

    def softmax(hd):
        t = t_s[hd]
        sink = sinks_ref[hd] * LOG2E
        m = jnp.maximum(jnp.max(t, axis=-1, keepdims=True), sink)
        e = jnp.exp2(t - m)
        den = jnp.sum(e, axis=-1, keepdims=True) + jnp.exp2(sink - m)
        pn = e * (1.0 / den)
        zero = jnp.zeros_like(pn)
        p_s[hd, :, 0:BLOCK] = jnp.where(upper_t, pn, zero).astype(_BF16)
        p_s[hd, :, BLOCK:2 * BLOCK] = jnp.where(upper_t, zero, pn).astype(_BF16)

    def weighted_values():
        for j in range(nblk):
            r0 = j * BLOCK
            for p in range(PAIRS):
                acc = None
                for par in range(2):
                    vv = v_s[2 * (p // 2) + par, r0:r0 + 2 * BLOCK, :]
                    o = jnp.dot(p_s[2 * p + par, r0:r0 + BLOCK, :], vv,
                                preferred_element_type=_F32)
                    acc = o if acc is None else acc + o
                att_s[r0:r0 + BLOCK, p * LANES:(p + 1) * LANES] = acc

    def spatial_gate():
        tril = col <= row
        for g in range(SGU_GROUPS):
            w = jnp.where(tril, ws_ref[g], 0.0).astype(_BF16)
            bias = bs_ref[g]
            c0 = g * LANES
            mixed = jnp.dot(w, vn_s[:, g * nblk * LANES:(g + 1) * nblk * LANES],
                            preferred_element_type=_F32)
            for c in range(nblk):
                r0 = c * BLOCK
                u_s[r0:r0 + BLOCK, c0:c0 + LANES] = u_s[r0:r0 + BLOCK, c0:c0 + LANES] * (
                    mixed[:, c * LANES:(c + 1) * LANES] + bias)

    vb_s[...] = _gelu_tanh(proj(_VB))
    scores()
    for hd in range(PAIRS):
        softmax(hd)
    u_s[...] = _gelu_tanh(proj(_UB))
    for hd in range(PAIRS, 2 * PAIRS):
        softmax(hd)
    weighted_values()
    ga = _silu_of_double(proj(_GA))
    vn = _layernorm(vb_s[...], vng_ref[...], vnb_ref[...]).astype(_BF16)
    for g in range(SGU_GROUPS):
        for c in range(nblk):
            vn_s[:, (g * nblk + c) * LANES:(g * nblk + c + 1) * LANES] = (
                vn[c * BLOCK:(c + 1) * BLOCK, g * LANES:(g + 1) * LANES])
    ya = (att_s[...] * ga).astype(_BF16)
    pa = jnp.dot(ya, pa_ref[:, 0:D_MODEL], preferred_element_type=_F32)
    m_s[...] = (jnp.tanh(proj(_RA)) + 1.0) * pa
    spatial_gate()
    yb = (u_s[...] * _silu_of_double(proj(_GB))).astype(_BF16)
    pb = jnp.dot(yb, pb_ref[:, 0:D_MODEL], preferred_element_type=_F32)
    mb_s[...] = (m_s[...] + (jnp.tanh(proj(_RB)) + 1.0) * pb).astype(_BF16)

    half = ts // 2
    for r0 in (0, half):
        out = jnp.dot(mb_s[r0:r0 + half, :], wout_ref[:, 0:D_MODEL],
                      preferred_element_type=_F32)
        o_ref[0, r0:r0 + half, :] = _layernorm(res_s[r0:r0 + half, :] + out,
                                               g_ref[...], b_ref[...])


def _resident(shape):
    nd = len(shape)
    return pl.BlockSpec(shape, lambda b, s: (0,) * nd, pipeline_mode=pl.Buffered(1))


def _layer(x, sinks, ln_in_g, ln_in_b, w_in, b_in, vn_g, vn_b, w_s, b_s, p_a, p_b, w_out,
           b_out, ln_g, ln_b, *, apply_in_ln):
    batch, seq, d = x.shape
    ts = SEQ_TILE
    assert d == D_MODEL and seq % ts == 0 and ts % BLOCK == 0
    kernel = functools.partial(_layer_kernel, apply_in_ln=apply_in_ln, ts=ts)
    tile = pl.BlockSpec((1, ts, d), lambda b, s: (b, s, 0))
    dp = d + WEIGHT_COL_PAD
    in_specs = [
        pl.BlockSpec(memory_space=pltpu.SMEM),
        tile,
        _resident((1, d)), _resident((1, d)),
        _resident((d, IN_COLS)), _resident((1, IN_COLS)),
        _resident((1, SGU_WIDTH)), _resident((1, SGU_WIDTH)),
        _resident((SGU_GROUPS, BLOCK, BLOCK)), _resident((SGU_GROUPS, BLOCK, LANES)),
        _resident((ATTN_WIDTH, dp)), _resident((SGU_WIDTH, dp)),
        _resident((d, dp)), _resident((1, d)),
        _resident((1, d)), _resident((1, d)),
    ]
    scratch = [
        pltpu.VMEM((ts, d), _F32),
        pltpu.VMEM((ts, d), _BF16),
        pltpu.VMEM((ts, ATTN_WIDTH), _BF16),
        pltpu.VMEM((4, BLOCK + ts, LANES), _BF16),
        pltpu.VMEM((4, BLOCK + ts, LANES), _BF16),
        pltpu.VMEM((2 * PAIRS, ts, BLOCK), _F32),
        pltpu.VMEM((2 * PAIRS, ts, 2 * BLOCK), _BF16),
        pltpu.VMEM((ts, ATTN_WIDTH), _F32),
        pltpu.VMEM((ts, SGU_WIDTH), _F32),
        pltpu.VMEM((ts, SGU_WIDTH), _F32),
        pltpu.VMEM((BLOCK, (ts // BLOCK) * SGU_WIDTH), _BF16),
        pltpu.VMEM((ts, d), _F32),
        pltpu.VMEM((ts, d), _BF16),
    ]
    return pl.pallas_call(
        kernel,
        grid=(batch, seq // ts),
        in_specs=in_specs,
        out_specs=tile,
        out_shape=jax.ShapeDtypeStruct(x.shape, x.dtype),
        scratch_shapes=scratch,
        compiler_params=pltpu.CompilerParams(
            dimension_semantics=("arbitrary", "arbitrary"),
            vmem_limit_bytes=VMEM_LIMIT_BYTES),
        name="hybrid_layer_in_ln" if apply_in_ln else "hybrid_layer",
    )(sinks, x, ln_in_g, ln_in_b, w_in, b_in, vn_g, vn_b, w_s, b_s, p_a, p_b, w_out, b_out,
      ln_g, ln_b)


def kernel(x, ln_in_g, ln_in_b, w_in, b_in, sinks, vn_g, vn_b, w_s, b_s, p_a, p_b, w_out,
           b_out, ln_g, ln_b):
    row = lambda a: a.reshape(1, -1)
    pad_cols = lambda w: jnp.pad(w.astype(_BF16), ((0, 0), (0, WEIGHT_COL_PAD)))
    col_scale = jnp.concatenate(
        [jnp.full((n,), s, _F32) for n, s in zip(_SPLITS, _COL_SCALE)])
    for l in range(DEPTH):
        bias = jnp.broadcast_to(b_s[l][:, :, None], (SGU_GROUPS, BLOCK, LANES))
        x = _layer(x, sinks[l], row(ln_in_g), row(ln_in_b),
                   (w_in[l] * col_scale).astype(_BF16), row(b_in[l] * col_scale),
                   row(vn_g[l]), row(vn_b[l]), w_s[l], bias,
                   pad_cols(0.5 * p_a[l]), pad_cols(0.5 * p_b[l]),
                   pad_cols(w_out[l]), row(b_out[l]), row(ln_g[l]), row(ln_b[l]),
                   apply_in_ln=(l == 0))
    return x
```

```python
import functools

import jax
import jax.numpy as jnp
from jax import lax
from jax.experimental import pallas as pl
from jax.experimental.pallas import tpu as pltpu

D_MODEL = 1024
DEPTH = 2
HEAD_DIM = 64
ATTN_WIDTH = 512
KV_WIDTH = 128
BLOCK = 128
SGU_WIDTH = 512
SGU_GROUPS = 4
ALPHA = (2.0 * DEPTH) ** 0.25
LN_EPS = 1e-5
LOG2E = 1.4426950408889634
LANES = 128
PAIRS = ATTN_WIDTH // LANES

_SPLITS = (ATTN_WIDTH, KV_WIDTH, KV_WIDTH, ATTN_WIDTH, SGU_WIDTH, SGU_WIDTH, SGU_WIDTH,
           D_MODEL, D_MODEL)
_OFFS = tuple(sum(_SPLITS[:i]) for i in range(len(_SPLITS) + 1))
IN_COLS = _OFFS[-1]
(_Q, _K, _V, _GA, _UB, _VB, _GB, _RA, _RB) = tuple(
    (_OFFS[i], _OFFS[i + 1]) for i in range(len(_SPLITS)))
_COL_SCALE = (HEAD_DIM ** -0.5 * LOG2E, 1.0, 1.0, 0.5, 1.0, 1.0, 0.5, 0.5, 0.5)

SEQ_TILE = 512
VMEM_LIMIT_BYTES = 56 * 1024 * 1024
WEIGHT_COL_PAD = LANES

_F32 = jnp.float32
_BF16 = jnp.bfloat16
_NEG = float(jnp.finfo(jnp.float32).min)


def _layernorm(x, g, b):
    mu = jnp.mean(x, axis=-1, keepdims=True)
    xc = x - mu
    var = jnp.mean(xc * xc, axis=-1, keepdims=True)
    return xc * lax.rsqrt(var + LN_EPS) * g + b


def _silu_of_double(h):
    return h * jnp.tanh(h) + h


def _gelu_tanh(x):
    c = 0.7978845608028654
    hx = 0.5 * x
    return hx * jnp.tanh(x * ((c * 0.044715) * (x * x) + c)) + hx


def _layer_kernel(sinks_ref, x_ref, lng_ref, lnb_ref, win_ref, bin_ref, vng_ref, vnb_ref,
                  ws_ref, bs_ref, pa_ref, pb_ref, wout_ref, bout_ref, g_ref, b_ref,
                  o_ref,
                  res_s, xb_s, q_s, k_s, v_s, t_s, p_s, att_s, u_s, vb_s, vn_s, m_s, mb_s,
                  *, apply_in_ln, ts):
    nblk = ts // BLOCK
    first_tile = pl.program_id(1) == 0

    x = x_ref[0]
    if apply_in_ln:
        x = _layernorm(x, lng_ref[...], lnb_ref[...])
    res_s[...] = ALPHA * x + bout_ref[...]
    xb_s[...] = x.astype(_BF16)

    def proj(rng):
        lo, hi = rng
        return (jnp.dot(xb_s[...], win_ref[:, lo:hi], preferred_element_type=_F32)
                + bin_ref[:, lo:hi])

    lane = lax.broadcasted_iota(jnp.int32, (ts, LANES), 1)
    left = lane < HEAD_DIM

    q_s[...] = proj(_Q).astype(_BF16)

    @pl.when(first_tile)
    def _():
        k_s[:, 0:BLOCK, :] = jnp.zeros((4, BLOCK, LANES), _BF16)
        v_s[:, 0:BLOCK, :] = jnp.zeros((4, BLOCK, LANES), _BF16)

    @pl.when(jnp.logical_not(first_tile))
    def _():
        k_s[:, 0:BLOCK, :] = k_s[:, ts:ts + BLOCK, :]
        v_s[:, 0:BLOCK, :] = v_s[:, ts:ts + BLOCK, :]

    def spread(t, dst):
        rot = pltpu.roll(t, HEAD_DIM, axis=1)
        zero = jnp.zeros_like(t)
        dst[0, BLOCK:BLOCK + ts, :] = jnp.where(left, t, zero).astype(_BF16)
        dst[1, BLOCK:BLOCK + ts, :] = jnp.where(left, zero, rot).astype(_BF16)
        dst[2, BLOCK:BLOCK + ts, :] = jnp.where(left, rot, zero).astype(_BF16)
        dst[3, BLOCK:BLOCK + ts, :] = jnp.where(left, zero, t).astype(_BF16)

    kv = proj((_K[0], _V[1]))
    spread(kv[:, 0:KV_WIDTH], k_s)
    spread(kv[:, KV_WIDTH:2 * KV_WIDTH], v_s)

    row = lax.broadcasted_iota(jnp.int32, (BLOCK, BLOCK), 0)
    col = lax.broadcasted_iota(jnp.int32, (BLOCK, BLOCK), 1)
    upper = col > row

    def scores():
        for j in range(nblk):
            r0 = j * BLOCK
            for p in range(PAIRS):
                q2 = q_s[r0:r0 + BLOCK, p * LANES:(p + 1) * LANES]
                for par in range(2):
                    kk = k_s[2 * (p // 2) + par, r0:r0 + 2 * BLOCK, :]
                    s = lax.dot_general(q2, kk, (((1,), (1,)), ((), ())),
                                        preferred_element_type=_F32)
                    t = jnp.where(upper, s[:, 0:BLOCK], s[:, BLOCK:2 * BLOCK])
                    if j == 0:
                        t = jnp.where(jnp.logical_and(upper, first_tile), _NEG, t)
                    t_s[2 * p + par, r0:r0 + BLOCK, :] = t

    trow = lax.broadcasted_iota(jnp.int32, (ts, BLOCK), 0) & (BLOCK - 1)
    tcol = lax.broadcasted_iota(jnp.int32, (ts, BLOCK), 1)
    upper_t = tcol > trow

    def softmax(hd):
        t = t_s[hd]
        sink = sinks_ref[hd] * LOG2E
        m = jnp.maximum(jnp.max(t, axis=-1, keepdims=True), sink)
        e = jnp.exp2(t - m)
        den = jnp.sum(e, axis=-1, keepdims=True) + jnp.exp2(sink - m)
        pn = e * (1.0 / den)
        zero = jnp.zeros_like(pn)
        p_s[hd, :, 0:BLOCK] = jnp.where(upper_t, pn, zero).astype(_BF16)
        p_s[hd, :, BLOCK:2 * BLOCK] = jnp.where(upper_t, zero, pn).astype(_BF16)

    def weighted_values():
        for j in range(nblk):
            r0 = j * BLOCK
            for p in range(PAIRS):
                acc = None
                for par in range(2):
                    vv = v_s[2 * (p // 2) + par, r0:r0 + 2 * BLOCK, :]
                    o = jnp.dot(p_s[2 * p + par, r0:r0 + BLOCK, :], vv,
                                preferred_element_type=_F32)
                    acc = o if acc is None else acc + o
                att_s[r0:r0 + BLOCK, p * LANES:(p + 1) * LANES] = acc

    def spatial_gate():
        tril = col <= row
        for g in range(SGU_GROUPS):
            w = jnp.where(tril, ws_ref[g], 0.0).astype(_BF16)
            bias = bs_ref[g]
            c0 = g * LANES
            mixed = jnp.dot(w, vn_s[:, g * nblk * LANES:(g + 1) * nblk * LANES],
                            preferred_element_type=_F32)
            for c in range(nblk):
                r0 = c * BLOCK
                u_s[r0:r0 + BLOCK, c0:c0 + LANES] = u_s[r0:r0 + BLOCK, c0:c0 + LANES] * (
                    mixed[:, c * LANES:(c + 1) * LANES] + bias)

    vb_s[...] = _gelu_tanh(proj(_VB))
    scores()
    for hd in range(PAIRS):
        softmax(hd)
    u_s[...] = _gelu_tanh(proj(_UB))
    for hd in range(PAIRS, PAIRS + 2):
        softmax(hd)
    m_s[...] = jnp.tanh(proj(_RA)) + 1.0
    for hd in range(PAIRS + 2, 2 * PAIRS):
        softmax(hd)
    weighted_values()
    ga = _silu_of_double(proj(_GA))
    vn = _layernorm(vb_s[...], vng_ref[...], vnb_ref[...]).astype(_BF16)
    for g in range(SGU_GROUPS):
        for c in range(nblk):
            vn_s[:, (g * nblk + c) * LANES:(g * nblk + c + 1) * LANES] = (
                vn[c * BLOCK:(c + 1) * BLOCK, g * LANES:(g + 1) * LANES])
    ya = (att_s[...] * ga).astype(_BF16)
    pa = jnp.dot(ya, pa_ref[:, 0:D_MODEL], preferred_element_type=_F32)
    m_s[...] = m_s[...] * pa
    spatial_gate()
    yb = (u_s[...] * _silu_of_double(proj(_GB))).astype(_BF16)
    pb = jnp.dot(yb, pb_ref[:, 0:D_MODEL], preferred_element_type=_F32)
    mb_s[...] = (m_s[...] + (jnp.tanh(proj(_RB)) + 1.0) * pb).astype(_BF16)

    half = ts // 2
    for r0 in (0, half):
        out = jnp.dot(mb_s[r0:r0 + half, :], wout_ref[:, 0:D_MODEL],
                      preferred_element_type=_F32)
        o_ref[0, r0:r0 + half, :] = _layernorm(res_s[r0:r0 + half, :] + out,
                                               g_ref[...], b_ref[...])


def _resident(shape):
    nd = len(shape)
    return pl.BlockSpec(shape, lambda b, s: (0,) * nd, pipeline_mode=pl.Buffered(1))


def _layer(x, sinks, ln_in_g, ln_in_b, w_in, b_in, vn_g, vn_b, w_s, b_s, p_a, p_b, w_out,
           b_out, ln_g, ln_b, *, apply_in_ln):
    batch, seq, d = x.shape
    ts = SEQ_TILE
    assert d == D_MODEL and seq % ts == 0 and ts % BLOCK == 0
    kernel = functools.partial(_layer_kernel, apply_in_ln=apply_in_ln, ts=ts)
    tile = pl.BlockSpec((1, ts, d), lambda b, s: (b, s, 0))
    dp = d + WEIGHT_COL_PAD
    in_specs = [
        pl.BlockSpec(memory_space=pltpu.SMEM),
        tile,
        _resident((1, d)), _resident((1, d)),
        _resident((d, IN_COLS)), _resident((1, IN_COLS)),
        _resident((1, SGU_WIDTH)), _resident((1, SGU_WIDTH)),
        _resident((SGU_GROUPS, BLOCK, BLOCK)), _resident((SGU_GROUPS, BLOCK, LANES)),
        _resident((ATTN_WIDTH, dp)), _resident((SGU_WIDTH, dp)),
        _resident((d, dp)), _resident((1, d)),
        _resident((1, d)), _resident((1, d)),
    ]
    scratch = [
        pltpu.VMEM((ts, d), _F32),
        pltpu.VMEM((ts, d), _BF16),
        pltpu.VMEM((ts, ATTN_WIDTH), _BF16),
        pltpu.VMEM((4, BLOCK + ts, LANES), _BF16),
        pltpu.VMEM((4, BLOCK + ts, LANES), _BF16),
        pltpu.VMEM((2 * PAIRS, ts, BLOCK), _F32),
        pltpu.VMEM((2 * PAIRS, ts, 2 * BLOCK), _BF16),
        pltpu.VMEM((ts, ATTN_WIDTH), _F32),
        pltpu.VMEM((ts, SGU_WIDTH), _F32),
        pltpu.VMEM((ts, SGU_WIDTH), _F32),
        pltpu.VMEM((BLOCK, (ts // BLOCK) * SGU_WIDTH), _BF16),
        pltpu.VMEM((ts, d), _F32),
        pltpu.VMEM((ts, d), _BF16),
    ]
    return pl.pallas_call(
        kernel,
        grid=(batch, seq // ts),
        in_specs=in_specs,
        out_specs=tile,
        out_shape=jax.ShapeDtypeStruct(x.shape, x.dtype),
        scratch_shapes=scratch,
        compiler_params=pltpu.CompilerParams(
            dimension_semantics=("arbitrary", "arbitrary"),
            vmem_limit_bytes=VMEM_LIMIT_BYTES),
        name="hybrid_layer_in_ln" if apply_in_ln else "hybrid_layer",
    )(sinks, x, ln_in_g, ln_in_b, w_in, b_in, vn_g, vn_b, w_s, b_s, p_a, p_b, w_out, b_out,
      ln_g, ln_b)


def kernel(x, ln_in_g, ln_in_b, w_in, b_in, sinks, vn_g, vn_b, w_s, b_s, p_a, p_b, w_out,
           b_out, ln_g, ln_b):
    row = lambda a: a.reshape(1, -1)
    pad_cols = lambda w: jnp.pad(w.astype(_BF16), ((0, 0), (0, WEIGHT_COL_PAD)))
    col_scale = jnp.concatenate(
        [jnp.full((n,), s, _F32) for n, s in zip(_SPLITS, _COL_SCALE)])
    for l in range(DEPTH):
        bias = jnp.broadcast_to(b_s[l][:, :, None], (SGU_GROUPS, BLOCK, LANES))
        x = _layer(x, sinks[l], row(ln_in_g), row(ln_in_b),
                   (w_in[l] * col_scale).astype(_BF16), row(b_in[l] * col_scale),
                   row(vn_g[l]), row(vn_b[l]), w_s[l], bias,
                   pad_cols(0.5 * p_a[l]), pad_cols(0.5 * p_b[l]),
                   pad_cols(w_out[l]), row(b_out[l]), row(ln_g[l]), row(ln_b[l]),
                   apply_in_ln=(l == 0))
    return x
```

```python
import functools

import jax
import jax.numpy as jnp
from jax import lax
from jax.experimental import pallas as pl
from jax.experimental.pallas import tpu as pltpu

D_MODEL = 1024
DEPTH = 2
HEAD_DIM = 64
ATTN_WIDTH = 512
KV_WIDTH = 128
BLOCK = 128
SGU_WIDTH = 512
SGU_GROUPS = 4
ALPHA = (2.0 * DEPTH) ** 0.25
LN_EPS = 1e-5
LOG2E = 1.4426950408889634
LANES = 128
PAIRS = ATTN_WIDTH // LANES

_SPLITS = (ATTN_WIDTH, KV_WIDTH, KV_WIDTH, ATTN_WIDTH, SGU_WIDTH, SGU_WIDTH, SGU_WIDTH,
           D_MODEL, D_MODEL)
_OFFS = tuple(sum(_SPLITS[:i]) for i in range(len(_SPLITS) + 1))
IN_COLS = _OFFS[-1]
(_Q, _K, _V, _GA, _UB, _VB, _GB, _RA, _RB) = tuple(
    (_OFFS[i], _OFFS[i + 1]) for i in range(len(_SPLITS)))
_COL_SCALE = (HEAD_DIM ** -0.5 * LOG2E, 1.0, 1.0, 0.5, 1.0, 1.0, 0.5, 0.5, 0.5)

SEQ_TILE = 512
VMEM_LIMIT_BYTES = 56 * 1024 * 1024
WEIGHT_COL_PAD = LANES

_F32 = jnp.float32
_BF16 = jnp.bfloat16
_NEG = float(jnp.finfo(jnp.float32).min)


def _layernorm(x, g, b):
    mu = jnp.mean(x, axis=-1, keepdims=True)
    xc = x - mu
    var = jnp.mean(xc * xc, axis=-1, keepdims=True)
    return xc * lax.rsqrt(var + LN_EPS) * g + b


def _silu_of_double(h):
    return h * jnp.tanh(h) + h


def _gelu_tanh(x):
    c = 0.7978845608028654
    hx = 0.5 * x
    return hx * jnp.tanh(x * ((c * 0.044715) * (x * x) + c)) + hx


def _layer_kernel(sinks_ref, x_ref, lng_ref, lnb_ref, win_ref, bin_ref, vng_ref, vnb_ref,
                  ws_ref, bs_ref, pa_ref, pb_ref, wout_ref, bout_ref, g_ref, b_ref,
                  o_ref,
                  res_s, xb_s, q_s, k_s, v_s, t_s, p_s, att_s, u_s, vb_s, vn_s, m_s, mb_s,
                  *, apply_in_ln, ts):
    nblk = ts // BLOCK
    first_tile = pl.program_id(1) == 0

    @pl.when(jnp.logical_and(pl.program_id(0) == 0, first_tile))
    def _():
        k_s[:, ts:ts + BLOCK, :] = jnp.zeros((4, BLOCK, LANES), _BF16)
        v_s[:, ts:ts + BLOCK, :] = jnp.zeros((4, BLOCK, LANES), _BF16)

    x = x_ref[0]
    if apply_in_ln:
        x = _layernorm(x, lng_ref[...], lnb_ref[...])
    res_s[...] = ALPHA * x + bout_ref[...]
    xb_s[...] = x.astype(_BF16)

    def proj(rng):
        lo, hi = rng
        return (jnp.dot(xb_s[...], win_ref[:, lo:hi], preferred_element_type=_F32)
                + bin_ref[:, lo:hi])

    lane = lax.broadcasted_iota(jnp.int32, (ts, LANES), 1)
    left = lane < HEAD_DIM

    q_s[...] = proj(_Q).astype(_BF16)

    k_s[:, 0:BLOCK, :] = k_s[:, ts:ts + BLOCK, :]
    v_s[:, 0:BLOCK, :] = v_s[:, ts:ts + BLOCK, :]

    def spread(t, dst):
        rot = pltpu.roll(t, HEAD_DIM, axis=1)
        zero = jnp.zeros_like(t)
        dst[0, BLOCK:BLOCK + ts, :] = jnp.where(left, t, zero).astype(_BF16)
        dst[1, BLOCK:BLOCK + ts, :] = jnp.where(left, zero, rot).astype(_BF16)
        dst[2, BLOCK:BLOCK + ts, :] = jnp.where(left, rot, zero).astype(_BF16)
        dst[3, BLOCK:BLOCK + ts, :] = jnp.where(left, zero, t).astype(_BF16)

    kv = proj((_K[0], _V[1]))
    spread(kv[:, 0:KV_WIDTH], k_s)
    spread(kv[:, KV_WIDTH:2 * KV_WIDTH], v_s)

    row = lax.broadcasted_iota(jnp.int32, (BLOCK, BLOCK), 0)
    col = lax.broadcasted_iota(jnp.int32, (BLOCK, BLOCK), 1)
    upper = col > row

    def scores():
        for j in range(nblk):
            r0 = j * BLOCK
            for p in range(PAIRS):
                q2 = q_s[r0:r0 + BLOCK, p * LANES:(p + 1) * LANES]
                for par in range(2):
                    kk = k_s[2 * (p // 2) + par, r0:r0 + 2 * BLOCK, :]
                    s = lax.dot_general(q2, kk, (((1,), (1,)), ((), ())),
                                        preferred_element_type=_F32)
                    t = jnp.where(upper, s[:, 0:BLOCK], s[:, BLOCK:2 * BLOCK])
                    if j == 0:
                        t = jnp.where(jnp.logical_and(upper, first_tile), _NEG, t)
                    t_s[2 * p + par, r0:r0 + BLOCK, :] = t

    trow = lax.broadcasted_iota(jnp.int32, (ts, BLOCK), 0) & (BLOCK - 1)
    tcol = lax.broadcasted_iota(jnp.int32, (ts, BLOCK), 1)
    upper_t = tcol > trow

    def softmax(hd):
        t = t_s[hd]
        sink = sinks_ref[hd] * LOG2E
        m = jnp.maximum(jnp.max(t, axis=-1, keepdims=True), sink)
        e = jnp.exp2(t - m)
        den = jnp.sum(e, axis=-1, keepdims=True) + jnp.exp2(sink - m)
        pn = e * (1.0 / den)
        zero = jnp.zeros_like(pn)
        p_s[hd, :, 0:BLOCK] = jnp.where(upper_t, pn, zero).astype(_BF16)
        p_s[hd, :, BLOCK:2 * BLOCK] = jnp.where(upper_t, zero, pn).astype(_BF16)

    def weighted_values():
        for j in range(nblk):
            r0 = j * BLOCK
            for p in range(PAIRS):
                acc = None
                for par in range(2):
                    vv = v_s[2 * (p // 2) + par, r0:r0 + 2 * BLOCK, :]
                    o = jnp.dot(p_s[2 * p + par, r0:r0 + BLOCK, :], vv,
                                preferred_element_type=_F32)
                    acc = o if acc is None else acc + o
                att_s[r0:r0 + BLOCK, p * LANES:(p + 1) * LANES] = acc

    def spatial_gate():
        tril = col <= row
        for g in range(SGU_GROUPS):
            w = jnp.where(tril, ws_ref[g], 0.0).astype(_BF16)
            bias = bs_ref[g]
            c0 = g * LANES
            mixed = jnp.dot(w, vn_s[:, g * nblk * LANES:(g + 1) * nblk * LANES],
                            preferred_element_type=_F32)
            for c in range(nblk):
                r0 = c * BLOCK
                u_s[r0:r0 + BLOCK, c0:c0 + LANES] = u_s[r0:r0 + BLOCK, c0:c0 + LANES] * (
                    mixed[:, c * LANES:(c + 1) * LANES] + bias)

    vb_s[...] = _gelu_tanh(proj(_VB))
    scores()
    for hd in range(PAIRS):
        softmax(hd)
    u_s[...] = _gelu_tanh(proj(_UB))
    for hd in range(PAIRS, PAIRS + 2):
        softmax(hd)
    m_s[...] = jnp.tanh(proj(_RA)) + 1.0
    for hd in range(PAIRS + 2, 2 * PAIRS):
        softmax(hd)
    weighted_values()
    ga = _silu_of_double(proj(_GA))
    vn = _layernorm(vb_s[...], vng_ref[...], vnb_ref[...]).astype(_BF16)
    for g in range(SGU_GROUPS):
        for c in range(nblk):
            vn_s[:, (g * nblk + c) * LANES:(g * nblk + c + 1) * LANES] = (
                vn[c * BLOCK:(c + 1) * BLOCK, g * LANES:(g + 1) * LANES])
    ya = (att_s[...] * ga).astype(_BF16)
    pa = jnp.dot(ya, pa_ref[:, 0:D_MODEL], preferred_element_type=_F32)
    m_s[...] = m_s[...] * pa
    spatial_gate()
    yb = (u_s[...] * _silu_of_double(proj(_GB))).astype(_BF16)
    pb = jnp.dot(yb, pb_ref[:, 0:D_MODEL], preferred_element_type=_F32)
    mb_s[...] = (m_s[...] + (jnp.tanh(proj(_RB)) + 1.0) * pb).astype(_BF16)

    half = ts // 2
    for r0 in (0, half):
        out = jnp.dot(mb_s[r0:r0 + half, :], wout_ref[:, 0:D_MODEL],
                      preferred_element_type=_F32)
        o_ref[0, r0:r0 + half, :] = _layernorm(res_s[r0:r0 + half, :] + out,
                                               g_ref[...], b_ref[...])


def _resident(shape):
    nd = len(shape)
    return pl.BlockSpec(shape, lambda b, s: (0,) * nd, pipeline_mode=pl.Buffered(1))


def _layer(x, sinks, ln_in_g, ln_in_b, w_in, b_in, vn_g, vn_b, w_s, b_s, p_a, p_b, w_out,
           b_out, ln_g, ln_b, *, apply_in_ln):
    batch, seq, d = x.shape
    ts = SEQ_TILE
    assert d == D_MODEL and seq % ts == 0 and ts % BLOCK == 0
    kernel = functools.partial(_layer_kernel, apply_in_ln=apply_in_ln, ts=ts)
    tile = pl.BlockSpec((1, ts, d), lambda b, s: (b, s, 0))
    dp = d + WEIGHT_COL_PAD
    in_specs = [
        pl.BlockSpec(memory_space=pltpu.SMEM),
        tile,
        _resident((1, d)), _resident((1, d)),
        _resident((d, IN_COLS)), _resident((1, IN_COLS)),
        _resident((1, SGU_WIDTH)), _resident((1, SGU_WIDTH)),
        _resident((SGU_GROUPS, BLOCK, BLOCK)), _resident((SGU_GROUPS, BLOCK, LANES)),
        _resident((ATTN_WIDTH, dp)), _resident((SGU_WIDTH, dp)),
        _resident((d, dp)), _resident((1, d)),
        _resident((1, d)), _resident((1, d)),
    ]
    scratch = [
        pltpu.VMEM((ts, d), _F32),
        pltpu.VMEM((ts, d), _BF16),
        pltpu.VMEM((ts, ATTN_WIDTH), _BF16),
        pltpu.VMEM((4, BLOCK + ts, LANES), _BF16),
        pltpu.VMEM((4, BLOCK + ts, LANES), _BF16),
        pltpu.VMEM((2 * PAIRS, ts, BLOCK), _F32),
        pltpu.VMEM((2 * PAIRS, ts, 2 * BLOCK), _BF16),
        pltpu.VMEM((ts, ATTN_WIDTH), _F32),
        pltpu.VMEM((ts, SGU_WIDTH), _F32),
        pltpu.VMEM((ts, SGU_WIDTH), _F32),
        pltpu.VMEM((BLOCK, (ts // BLOCK) * SGU_WIDTH), _BF16),
        pltpu.VMEM((ts, d), _F32),
        pltpu.VMEM((ts, d), _BF16),
    ]
    return pl.pallas_call(
        kernel,
        grid=(batch, seq // ts),
        in_specs=in_specs,
        out_specs=tile,
        out_shape=jax.ShapeDtypeStruct(x.shape, x.dtype),
        scratch_shapes=scratch,
        compiler_params=pltpu.CompilerParams(
            dimension_semantics=("arbitrary", "arbitrary"),
            vmem_limit_bytes=VMEM_LIMIT_BYTES),
        name="hybrid_layer_in_ln" if apply_in_ln else "hybrid_layer",
    )(sinks, x, ln_in_g, ln_in_b, w_in, b_in, vn_g, vn_b, w_s, b_s, p_a, p_b, w_out, b_out,
      ln_g, ln_b)


def kernel(x, ln_in_g, ln_in_b, w_in, b_in, sinks, vn_g, vn_b, w_s, b_s, p_a, p_b, w_out,
           b_out, ln_g, ln_b):
    row = lambda a: a.reshape(1, -1)
    pad_cols = lambda w: jnp.pad(w.astype(_BF16), ((0, 0), (0, WEIGHT_COL_PAD)))
    col_scale = jnp.concatenate(
        [jnp.full((n,), s, _F32) for n, s in zip(_SPLITS, _COL_SCALE)])
    for l in range(DEPTH):
        bias = jnp.broadcast_to(b_s[l][:, :, None], (SGU_GROUPS, BLOCK, LANES))
        x = _layer(x, sinks[l], row(ln_in_g), row(ln_in_b),
                   (w_in[l] * col_scale).astype(_BF16), row(b_in[l] * col_scale),
                   row(vn_g[l]), row(vn_b[l]), w_s[l], bias,
                   pad_cols(0.5 * p_a[l]), pad_cols(0.5 * p_b[l]),
                   pad_cols(w_out[l]), row(b_out[l]), row(ln_g[l]), row(ln_b[l]),
                   apply_in_ln=(l == 0))
    return x
```

```python
import functools

import jax
import jax.numpy as jnp
from jax import lax
from jax.experimental import pallas as pl
from jax.experimental.pallas import tpu as pltpu

D_MODEL = 1024
DEPTH = 2
HEAD_DIM = 64
ATTN_WIDTH = 512
KV_WIDTH = 128
BLOCK = 128
SGU_WIDTH = 512
SGU_GROUPS = 4
ALPHA = (2.0 * DEPTH) ** 0.25
LN_EPS = 1e-5
LOG2E = 1.4426950408889634
LANES = 128
PAIRS = ATTN_WIDTH // LANES

_SPLITS = (ATTN_WIDTH, KV_WIDTH, KV_WIDTH, ATTN_WIDTH, SGU_WIDTH, SGU_WIDTH, SGU_WIDTH,
           D_MODEL, D_MODEL)
_OFFS = tuple(sum(_SPLITS[:i]) for i in range(len(_SPLITS) + 1))
IN_COLS = _OFFS[-1]
(_Q, _K, _V, _GA, _UB, _VB, _GB, _RA, _RB) = tuple(
    (_OFFS[i], _OFFS[i + 1]) for i in range(len(_SPLITS)))
_COL_SCALE = (HEAD_DIM ** -0.5 * LOG2E, 1.0, 1.0, 0.5, 1.0, 1.0, 0.5, 0.5, 0.5)

SEQ_TILE = 512
VMEM_LIMIT_BYTES = 56 * 1024 * 1024
WEIGHT_COL_PAD = LANES

_F32 = jnp.float32
_BF16 = jnp.bfloat16
_NEG = float(jnp.finfo(jnp.float32).min)


def _layernorm(x, g, b):
    mu = jnp.mean(x, axis=-1, keepdims=True)
    xc = x - mu
    var = jnp.mean(xc * xc, axis=-1, keepdims=True)
    return xc * lax.rsqrt(var + LN_EPS) * g + b


def _silu_of_double(h):
    return h * jnp.tanh(h) + h


def _gelu_tanh(x):
    c = 0.7978845608028654
    hx = 0.5 * x
    return hx * jnp.tanh(x * ((c * 0.044715) * (x * x) + c)) + hx


def _layer_kernel(sinks_ref, x_ref, lng_ref, lnb_ref, win_ref, bin_ref, vng_ref, vnb_ref,
                  ws_ref, bs_ref, pa_ref, pb_ref, wout_ref, bout_ref, g_ref, b_ref,
                  o_ref,
                  res_s, xb_s, q_s, k_s, v_s, t_s, p_s, att_s, u_s, vb_s, vn_s, m_s, mb_s,
                  *, apply_in_ln, ts):
    nblk = ts // BLOCK
    first_tile = pl.program_id(1) == 0

    @pl.when(jnp.logical_and(pl.program_id(0) == 0, first_tile))
    def _():
        k_s[:, :, ts:ts + BLOCK] = jnp.zeros((4, LANES, BLOCK), _BF16)
        v_s[:, ts:ts + BLOCK, :] = jnp.zeros((4, BLOCK, LANES), _BF16)

    x = x_ref[0]
    if apply_in_ln:
        x = _layernorm(x, lng_ref[...], lnb_ref[...])
    res_s[...] = ALPHA * x + bout_ref[...]
    xb_s[...] = x.astype(_BF16)

    def proj(rng):
        lo, hi = rng
        return (jnp.dot(xb_s[...], win_ref[:, lo:hi], preferred_element_type=_F32)
                + bin_ref[:, lo:hi])

    lane = lax.broadcasted_iota(jnp.int32, (ts, LANES), 1)
    left = lane < HEAD_DIM

    q_s[...] = proj(_Q).astype(_BF16)

    k_s[:, :, 0:BLOCK] = k_s[:, :, ts:ts + BLOCK]
    v_s[:, 0:BLOCK, :] = v_s[:, ts:ts + BLOCK, :]

    def spread(t, dst):
        rot = pltpu.roll(t, HEAD_DIM, axis=1)
        zero = jnp.zeros_like(t)
        dst[0, BLOCK:BLOCK + ts, :] = jnp.where(left, t, zero).astype(_BF16)
        dst[1, BLOCK:BLOCK + ts, :] = jnp.where(left, zero, rot).astype(_BF16)
        dst[2, BLOCK:BLOCK + ts, :] = jnp.where(left, rot, zero).astype(_BF16)
        dst[3, BLOCK:BLOCK + ts, :] = jnp.where(left, zero, t).astype(_BF16)

    kv = proj((_K[0], _V[1]))
    spread(kv[:, KV_WIDTH:2 * KV_WIDTH], v_s)
    k = kv[:, 0:KV_WIDTH]
    kt = k.T
    krt = pltpu.roll(k, HEAD_DIM, axis=1).T
    top = lax.broadcasted_iota(jnp.int32, (LANES, ts), 0) < HEAD_DIM
    zt = jnp.zeros_like(kt)
    k_s[0, :, BLOCK:BLOCK + ts] = jnp.where(top, kt, zt).astype(_BF16)
    k_s[1, :, BLOCK:BLOCK + ts] = jnp.where(top, zt, krt).astype(_BF16)
    k_s[2, :, BLOCK:BLOCK + ts] = jnp.where(top, krt, zt).astype(_BF16)
    k_s[3, :, BLOCK:BLOCK + ts] = jnp.where(top, zt, kt).astype(_BF16)

    row = lax.broadcasted_iota(jnp.int32, (BLOCK, BLOCK), 0)
    col = lax.broadcasted_iota(jnp.int32, (BLOCK, BLOCK), 1)
    upper = col > row

    def scores():
        for j in range(nblk):
            r0 = j * BLOCK
            for p in range(PAIRS):
                q2 = q_s[r0:r0 + BLOCK, p * LANES:(p + 1) * LANES]
                for par in range(2):
                    kk = k_s[2 * (p // 2) + par, :, r0:r0 + 2 * BLOCK]
                    s = jnp.dot(q2, kk, preferred_element_type=_F32)
                    t = jnp.where(upper, s[:, 0:BLOCK], s[:, BLOCK:2 * BLOCK])
                    if j == 0:
                        t = jnp.where(jnp.logical_and(upper, first_tile), _NEG, t)
                    t_s[2 * p + par, r0:r0 + BLOCK, :] = t

    trow = lax.broadcasted_iota(jnp.int32, (ts, BLOCK), 0) & (BLOCK - 1)
    tcol = lax.broadcasted_iota(jnp.int32, (ts, BLOCK), 1)
    upper_t = tcol > trow

    def softmax(hd):
        t = t_s[hd]
        sink = sinks_ref[hd] * LOG2E
        m = jnp.maximum(jnp.max(t, axis=-1, keepdims=True), sink)
        e = jnp.exp2(t - m)
        den = jnp.sum(e, axis=-1, keepdims=True) + jnp.exp2(sink - m)
        pn = e * (1.0 / den)
        zero = jnp.zeros_like(pn)
        p_s[hd, :, 0:BLOCK] = jnp.where(upper_t, pn, zero).astype(_BF16)
        p_s[hd, :, BLOCK:2 * BLOCK] = jnp.where(upper_t, zero, pn).astype(_BF16)

    def weighted_values():
        for j in range(nblk):
            r0 = j * BLOCK
            for p in range(PAIRS):
                acc = None
                for par in range(2):
                    vv = v_s[2 * (p // 2) + par, r0:r0 + 2 * BLOCK, :]
                    o = jnp.dot(p_s[2 * p + par, r0:r0 + BLOCK, :], vv,
                                preferred_element_type=_F32)
                    acc = o if acc is None else acc + o
                att_s[r0:r0 + BLOCK, p * LANES:(p + 1) * LANES] = acc

    def spatial_gate():
        tril = col <= row
        for g in range(SGU_GROUPS):
            w = jnp.where(tril, ws_ref[g], 0.0).astype(_BF16)
            bias = bs_ref[g]
            c0 = g * LANES
            mixed = jnp.dot(w, vn_s[:, g * nblk * LANES:(g + 1) * nblk * LANES],
                            preferred_element_type=_F32)
            for c in range(nblk):
                r0 = c * BLOCK
                u_s[r0:r0 + BLOCK, c0:c0 + LANES] = u_s[r0:r0 + BLOCK, c0:c0 + LANES] * (
                    mixed[:, c * LANES:(c + 1) * LANES] + bias)

    vb_s[...] = _gelu_tanh(proj(_VB))
    scores()
    for hd in range(PAIRS):
        softmax(hd)
    u_s[...] = _gelu_tanh(proj(_UB))
    for hd in range(PAIRS, PAIRS + 2):
        softmax(hd)
    m_s[...] = jnp.tanh(proj(_RA)) + 1.0
    for hd in range(PAIRS + 2, 2 * PAIRS):
        softmax(hd)
    weighted_values()
    ga = _silu_of_double(proj(_GA))
    vn = _layernorm(vb_s[...], vng_ref[...], vnb_ref[...]).astype(_BF16)
    for g in range(SGU_GROUPS):
        for c in range(nblk):
            vn_s[:, (g * nblk + c) * LANES:(g * nblk + c + 1) * LANES] = (
                vn[c * BLOCK:(c + 1) * BLOCK, g * LANES:(g + 1) * LANES])
    ya = (att_s[...] * ga).astype(_BF16)
    pa = jnp.dot(ya, pa_ref[:, 0:D_MODEL], preferred_element_type=_F32)
    m_s[...] = m_s[...] * pa
    spatial_gate()
    yb = (u_s[...] * _silu_of_double(proj(_GB))).astype(_BF16)
    pb = jnp.dot(yb, pb_ref[:, 0:D_MODEL], preferred_element_type=_F32)
    mb_s[...] = (m_s[...] + (jnp.tanh(proj(_RB)) + 1.0) * pb).astype(_BF16)

    half = ts // 2
    for r0 in (0, half):
        out = jnp.dot(mb_s[r0:r0 + half, :], wout_ref[:, 0:D_MODEL],
                      preferred_element_type=_F32)
        o_ref[0, r0:r0 + half, :] = _layernorm(res_s[r0:r0 + half, :] + out,
                                               g_ref[...], b_ref[...])


def _resident(shape):
    nd = len(shape)
    return pl.BlockSpec(shape, lambda b, s: (0,) * nd, pipeline_mode=pl.Buffered(1))


def _layer(x, sinks, ln_in_g, ln_in_b, w_in, b_in, vn_g, vn_b, w_s, b_s, p_a, p_b, w_out,
           b_out, ln_g, ln_b, *, apply_in_ln):
    batch, seq, d = x.shape
    ts = SEQ_TILE
    assert d == D_MODEL and seq % ts == 0 and ts % BLOCK == 0
    kernel = functools.partial(_layer_kernel, apply_in_ln=apply_in_ln, ts=ts)
    tile = pl.BlockSpec((1, ts, d), lambda b, s: (b, s, 0))
    dp = d + WEIGHT_COL_PAD
    in_specs = [
        pl.BlockSpec(memory_space=pltpu.SMEM),
        tile,
        _resident((1, d)), _resident((1, d)),
        _resident((d, IN_COLS)), _resident((1, IN_COLS)),
        _resident((1, SGU_WIDTH)), _resident((1, SGU_WIDTH)),
        _resident((SGU_GROUPS, BLOCK, BLOCK)), _resident((SGU_GROUPS, BLOCK, LANES)),
        _resident((ATTN_WIDTH, dp)), _resident((SGU_WIDTH, dp)),
        _resident((d, dp)), _resident((1, d)),
        _resident((1, d)), _resident((1, d)),
    ]
    scratch = [
        pltpu.VMEM((ts, d), _F32),
        pltpu.VMEM((ts, d), _BF16),
        pltpu.VMEM((ts, ATTN_WIDTH), _BF16),
        pltpu.VMEM((4, LANES, BLOCK + ts), _BF16),
        pltpu.VMEM((4, BLOCK + ts, LANES), _BF16),
        pltpu.VMEM((2 * PAIRS, ts, BLOCK), _F32),
        pltpu.VMEM((2 * PAIRS, ts, 2 * BLOCK), _BF16),
        pltpu.VMEM((ts, ATTN_WIDTH), _F32),
        pltpu.VMEM((ts, SGU_WIDTH), _F32),
        pltpu.VMEM((ts, SGU_WIDTH), _F32),
        pltpu.VMEM((BLOCK, (ts // BLOCK) * SGU_WIDTH), _BF16),
        pltpu.VMEM((ts, d), _F32),
        pltpu.VMEM((ts, d), _BF16),
    ]
    return pl.pallas_call(
        kernel,
        grid=(batch, seq // ts),
        in_specs=in_specs,
        out_specs=tile,
        out_shape=jax.ShapeDtypeStruct(x.shape, x.dtype),
        scratch_shapes=scratch,
        compiler_params=pltpu.CompilerParams(
            dimension_semantics=("arbitrary", "arbitrary"),
            vmem_limit_bytes=VMEM_LIMIT_BYTES),
        name="hybrid_layer_in_ln" if apply_in_ln else "hybrid_layer",
    )(sinks, x, ln_in_g, ln_in_b, w_in, b_in, vn_g, vn_b, w_s, b_s, p_a, p_b, w_out, b_out,
      ln_g, ln_b)


def kernel(x, ln_in_g, ln_in_b, w_in, b_in, sinks, vn_g, vn_b, w_s, b_s, p_a, p_b, w_out,
           b_out, ln_g, ln_b):
    row = lambda a: a.reshape(1, -1)
    pad_cols = lambda w: jnp.pad(w.astype(_BF16), ((0, 0), (0, WEIGHT_COL_PAD)))
    col_scale = jnp.concatenate(
        [jnp.full((n,), s, _F32) for n, s in zip(_SPLITS, _COL_SCALE)])
    for l in range(DEPTH):
        bias = jnp.broadcast_to(b_s[l][:, :, None], (SGU_GROUPS, BLOCK, LANES))
        x = _layer(x, sinks[l], row(ln_in_g), row(ln_in_b),
                   (w_in[l] * col_scale).astype(_BF16), row(b_in[l] * col_scale),
                   row(vn_g[l]), row(vn_b[l]), w_s[l], bias,
                   pad_cols(0.5 * p_a[l]), pad_cols(0.5 * p_b[l]),
                   pad_cols(w_out[l]), row(b_out[l]), row(ln_g[l]), row(ln_b[l]),
                   apply_in_ln=(l == 0))
    return x
```
